```python
import jax, jax.numpy as jnp
from jax import lax
import numpy as np

D_MODEL = 2048
BATCH = 8
SEQ = 2048
DEPTH = 2
DEC_BATCH = 128
DEC_SEQ = 4
PAST_LEN = 8192
PAGE_SIZE = 128

HEAD_DIM = 64
ATT_WIDTH = D_MODEL // 2
ATT_HEADS = ATT_WIDTH // HEAD_DIM
KV_HEADS = ATT_HEADS // 8
GROUP = ATT_HEADS // KV_HEADS
KV_WIDTH = KV_HEADS * HEAD_DIM
WINDOW = 128
BLOCK = 128
SGU_WIDTH = D_MODEL - ATT_WIDTH
SGU_HEADS = 8
SGU_HEAD_DIM = SGU_WIDTH // SGU_HEADS
CHUNK = 128
MIX_WIDTH = ATT_WIDTH + SGU_WIDTH
IN_COLS = ATT_WIDTH + 2 * KV_WIDTH + 2 * SGU_WIDTH
D_FF = 5504
CONV_W = 3
EPS = 1e-6
NEG_INF = -1e30

kernel_name = 'hymba_swa_sink_sgu_convffn_step'


def _rms(x, g):
    xf = x.astype(jnp.float32)
    y = xf * lax.rsqrt(jnp.mean(xf * xf, axis=-1, keepdims=True) + EPS)
    return (y * g.astype(jnp.float32)).astype(x.dtype)


def _layernorm(x, g, b):
    xf = x.astype(jnp.float32)
    mu = jnp.mean(xf, axis=-1, keepdims=True)
    xc = xf - mu
    y = xc * lax.rsqrt(jnp.mean(xc * xc, axis=-1, keepdims=True) + EPS)
    return (y * g.astype(jnp.float32) + b.astype(jnp.float32)).astype(x.dtype)


def _alibi_slopes():
    h = jnp.arange(1, ATT_HEADS + 1, dtype=jnp.float32)
    return jnp.exp2(-8.0 * h / ATT_HEADS).reshape(KV_HEADS, GROUP)


def _project(xn, w_in, g_q, g_k, ln_g, ln_b):
    z = xn @ w_in
    lead = z.shape[:-1]
    c1 = ATT_WIDTH
    c2 = c1 + KV_WIDTH
    c3 = c2 + KV_WIDTH
    c4 = c3 + SGU_WIDTH
    q = _rms(z[..., :c1].reshape(lead + (ATT_HEADS, HEAD_DIM)), g_q)
    k = _rms(z[..., c1:c2].reshape(lead + (KV_HEADS, HEAD_DIM)), g_k)
    v = z[..., c2:c3].reshape(lead + (KV_HEADS, HEAD_DIM))
    u = jax.nn.gelu(z[..., c3:c4])
    vs = _layernorm(jax.nn.gelu(z[..., c4:]), ln_g, ln_b).reshape(lead + (SGU_HEADS, SGU_HEAD_DIM))
    return q, k, v, u, vs


def _band_attention(q, k, v, q_pos, k_pos, sinks):
    B, N, Tq = q.shape[:3]
    qg = q.reshape(B, N, Tq, KV_HEADS, GROUP, HEAD_DIM)
    s = jnp.einsum('bnqkgd,bnskd->bnkgqs', qg, k,
                   preferred_element_type=jnp.float32) * (HEAD_DIM ** -0.5)
    dist = q_pos[:, :, None] - k_pos[:, None, :]
    valid = (dist >= 0) & (dist < WINDOW) & (k_pos[:, None, :] >= 0)
    slopes = _alibi_slopes()[None, None, :, :, None, None]
    s = s - slopes * dist.astype(jnp.float32)[None, :, None, None]
    s = jnp.where(valid[None, :, None, None], s, NEG_INF)
    sink = jnp.broadcast_to(
        sinks.astype(jnp.float32).reshape(KV_HEADS, GROUP)[None, None, :, :, None, None],
        s.shape[:-1] + (1,))
    p = jax.nn.softmax(jnp.concatenate([s, sink], axis=-1), axis=-1)[..., :-1]
    o = jnp.einsum('bnkgqs,bnskd->bnqkgd', p.astype(v.dtype), v)
    return o.reshape(B, N, Tq, ATT_WIDTH)


def _sgu(u, vs, w_s, b_s):
    T = u.shape[2]
    w = jnp.tril(w_s)[:, :T, :T]
    mixed = jnp.einsum('hts,bcshe->bcthe', w.astype(vs.dtype), vs) + b_s[:, :T].T[:, :, None]
    return u * mixed


def _merge(att, sgu, g_oa, g_os, w_o):
    return jnp.concatenate([_rms(att, g_oa), _rms(sgu, g_os)], axis=-1) @ w_o


def _conv_ffn(xn, buf, w_up, conv_w, conv_b, w_down):
    h = xn @ w_up
    T = h.shape[1]
    hp = jnp.concatenate([buf.astype(h.dtype), h], axis=1)
    hc = conv_b + conv_w[0] * hp[:, 0:T]
    for j in range(1, CONV_W):
        hc = hc + conv_w[j] * hp[:, j:j + T]
    a, g = jnp.split(hc, 2, axis=-1)
    return (jax.nn.silu(g) * a) @ w_down, hp[:, -(CONV_W - 1):]


def _prompt_layer(x, p):
    (g_mix, w_in, g_q, g_k, sinks, ln_g, ln_b, w_s, b_s,
     g_oa, g_os, w_o, g_ffn, w_up, conv_w, conv_b, w_down) = p
    B, S, _ = x.shape
    nb = S // BLOCK
    q, k, v, u, vs = _project(_rms(x, g_mix), w_in, g_q, g_k, ln_g, ln_b)
    qb = q.reshape(B, nb, BLOCK, ATT_HEADS, HEAD_DIM)
    kb = k.reshape(B, nb, BLOCK, KV_HEADS, HEAD_DIM)
    vb = v.reshape(B, nb, BLOCK, KV_HEADS, HEAD_DIM)

    def band(t):
        prev = jnp.pad(t, ((0, 0), (1, 0), (0, 0), (0, 0), (0, 0)))[:, :-1]
        return jnp.concatenate([prev, t], axis=2)

    pos = jnp.arange(S, dtype=jnp.int32).reshape(nb, BLOCK)
    k_pos = jnp.concatenate([pos - BLOCK, pos], axis=-1)
    att = _band_attention(qb, band(kb), band(vb), pos, k_pos, sinks).reshape(B, S, ATT_WIDTH)
    nc = S // CHUNK
    sgu = _sgu(u.reshape(B, nc, CHUNK, SGU_HEADS, SGU_HEAD_DIM),
               vs.reshape(B, nc, CHUNK, SGU_HEADS, SGU_HEAD_DIM), w_s, b_s).reshape(B, S, SGU_WIDTH)
    x = x + _merge(att, sgu, g_oa, g_os, w_o)
    zero_buf = jnp.zeros((B, CONV_W - 1, 2 * D_FF), x.dtype)
    y, conv_state = _conv_ffn(_rms(x, g_ffn), zero_buf, w_up, conv_w, conv_b, w_down)
    return x + y, k[:, -WINDOW:], v[:, -WINDOW:], conv_state


def _sample_layer(x, k_buf, v_buf, conv_buf, p):
    (g_mix, w_in, g_q, g_k, sinks, ln_g, ln_b, w_s, b_s,
     g_oa, g_os, w_o, g_ffn, w_up, conv_w, conv_b, w_down) = p
    Bd, T, _ = x.shape
    q, k, v, u, vs = _project(_rms(x, g_mix), w_in, g_q, g_k, ln_g, ln_b)
    kc = jnp.concatenate([k_buf.astype(k.dtype), k], axis=1)
    vc = jnp.concatenate([v_buf.astype(v.dtype), v], axis=1)
    q_pos = (PAST_LEN + jnp.arange(T, dtype=jnp.int32))[None]
    k_pos = (PAST_LEN - WINDOW + jnp.arange(WINDOW + T, dtype=jnp.int32))[None]
    att = _band_attention(q[:, None], kc[:, None], vc[:, None], q_pos, k_pos, sinks).reshape(Bd, T, ATT_WIDTH)
    sgu = _sgu(u.reshape(Bd, 1, T, SGU_HEADS, SGU_HEAD_DIM), vs[:, None], w_s, b_s).reshape(Bd, T, SGU_WIDTH)
    x = x + _merge(att, sgu, g_oa, g_os, w_o)
    y, conv_state = _conv_ffn(_rms(x, g_ffn), conv_buf, w_up, conv_w, conv_b, w_down)
    return x + y, kc[:, -WINDOW:], vc[:, -WINDOW:], conv_state, vs


def setup_inputs(seed: int = 0) -> dict:
    key = jax.random.key(seed)
    ks = jax.random.split(key, 24)
    f32 = jnp.float32
    nrm = lambda k, shape, scale: jax.random.normal(k, shape, f32) * scale
    gain = lambda k, shape: 1.0 + 0.02 * jax.random.normal(k, shape, f32)
    return {
        'x_prompt': nrm(ks[0], (BATCH, SEQ, D_MODEL), 1.0),
        'x_sample': nrm(ks[1], (DEC_BATCH, DEC_SEQ, D_MODEL), 1.0),
        'cache_k_win': nrm(ks[2], (DEPTH, DEC_BATCH, WINDOW, KV_HEADS, HEAD_DIM), 1.0),
        'cache_v_win': nrm(ks[3], (DEPTH, DEC_BATCH, WINDOW, KV_HEADS, HEAD_DIM), 1.0),
        'state_ffn_conv': nrm(ks[4], (DEPTH, DEC_BATCH, CONV_W - 1, 2 * D_FF), 1.0),
        'norm_mix_g': gain(ks[5], (DEPTH, D_MODEL)),
        'w_in': nrm(ks[6], (DEPTH, D_MODEL, IN_COLS), D_MODEL ** -0.5),
        'q_norm_g': gain(ks[7], (DEPTH, HEAD_DIM)),
        'k_norm_g': gain(ks[8], (DEPTH, HEAD_DIM)),
        'attn_sinks': nrm(ks[9], (DEPTH, ATT_HEADS), 0.5),
        'sgu_ln_g': gain(ks[10], (DEPTH, SGU_WIDTH)),
        'sgu_ln_b': nrm(ks[11], (DEPTH, SGU_WIDTH), 0.02),
        'sgu_w': nrm(ks[12], (DEPTH, SGU_HEADS, CHUNK, CHUNK), CHUNK ** -0.5),
        'sgu_b': gain(ks[13], (DEPTH, SGU_HEADS, CHUNK)),
        'out_norm_att_g': gain(ks[14], (DEPTH, ATT_WIDTH)),
        'out_norm_sgu_g': gain(ks[15], (DEPTH, SGU_WIDTH)),
        'w_o': nrm(ks[16], (DEPTH, MIX_WIDTH, D_MODEL), 0.5 * MIX_WIDTH ** -0.5),
        'norm_ffn_g': gain(ks[17], (DEPTH, D_MODEL)),
        'w_up': nrm(ks[18], (DEPTH, D_MODEL, 2 * D_FF), D_MODEL ** -0.5),
        'conv_w': nrm(ks[19], (DEPTH, CONV_W, 2 * D_FF), CONV_W ** -0.5),
        'conv_b': nrm(ks[20], (DEPTH, 2 * D_FF), 0.02),
        'w_down': nrm(ks[21], (DEPTH, D_FF, D_MODEL), 0.5 * D_FF ** -0.5),
    }


def reference(x_prompt, x_sample, cache_k_win, cache_v_win, state_ffn_conv,
              norm_mix_g, w_in, q_norm_g, k_norm_g, attn_sinks, sgu_ln_g, sgu_ln_b,
              sgu_w, sgu_b, out_norm_att_g, out_norm_sgu_g, w_o, norm_ffn_g,
              w_up, conv_w, conv_b, w_down):
    xp, xs = x_prompt, x_sample
    kp_l, vp_l, cp_l, ks_l, vs_l, cs_l, sv_l = [], [], [], [], [], [], []
    for l in range(DEPTH):
        p = (norm_mix_g[l], w_in[l], q_norm_g[l], k_norm_g[l], attn_sinks[l], sgu_ln_g[l],
             sgu_ln_b[l], sgu_w[l], sgu_b[l], out_norm_att_g[l], out_norm_sgu_g[l], w_o[l],
             norm_ffn_g[l], w_up[l], conv_w[l], conv_b[l], w_down[l])
        xp, kp, vp, cp = _prompt_layer(xp, p)
        xs, kn, vn, cn, svn = _sample_layer(xs, cache_k_win[l], cache_v_win[l], state_ffn_conv[l], p)
        kp_l.append(kp); vp_l.append(vp); cp_l.append(cp)
        ks_l.append(kn); vs_l.append(vn); cs_l.append(cn); sv_l.append(svn)
    k_win_prompt = jnp.stack(kp_l)
    v_win_prompt = jnp.stack(vp_l)
    ffn_conv_prompt = jnp.stack(cp_l)
    k_win_sample = jnp.stack(ks_l)
    v_win_sample = jnp.stack(vs_l)
    ffn_conv_sample = jnp.stack(cs_l)
    sgu_v_sample = jnp.stack(sv_l)
    return (xp, xs, k_win_prompt, v_win_prompt, ffn_conv_prompt,
            k_win_sample, v_win_sample, ffn_conv_sample, sgu_v_sample)
```

```python
import functools

import jax
import jax.numpy as jnp
from jax import lax
from jax.experimental import pallas as pl
from jax.experimental.pallas import tpu as pltpu

F32 = jnp.float32
BF16 = jnp.bfloat16

HEAD_DIM = 64
ATT_HEADS = 16
KV_HEADS = 2
GROUP = ATT_HEADS // KV_HEADS
ATT_WIDTH = ATT_HEADS * HEAD_DIM
KV_WIDTH = KV_HEADS * HEAD_DIM
SGU_HEADS = 8
SGU_HEAD_DIM = 128
SGU_WIDTH = SGU_HEADS * SGU_HEAD_DIM
WINDOW = 128
BLOCK = 128
CONV_W = 3
EPS = 1e-6
NEG_INF = -1e30
LANES = 128
HALO = 16
NEW_PAD = 16
VMEM_LIMIT = 56 * 1024 * 1024

C_Q = ATT_WIDTH
C_K = C_Q + KV_WIDTH
C_V = C_K + KV_WIDTH
C_U = C_V + SGU_WIDTH


def _cparams(n_axes):
    return pltpu.CompilerParams(dimension_semantics=("arbitrary",) * n_axes,
                                vmem_limit_bytes=VMEM_LIMIT)


def _rms_rows(x, g):
    y = x * lax.rsqrt(jnp.mean(x * x, axis=-1, keepdims=True) + EPS)
    return y * g


def _rms_head_pairs(z, g2):
    lo = lax.broadcasted_iota(jnp.int32, z.shape, 1) < HEAD_DIM
    z2 = z * z
    s_lo = jnp.sum(jnp.where(lo, z2, 0.0), axis=-1, keepdims=True)
    s_hi = jnp.sum(jnp.where(lo, 0.0, z2), axis=-1, keepdims=True)
    inv = jnp.where(lo, lax.rsqrt(s_lo / HEAD_DIM + EPS), lax.rsqrt(s_hi / HEAD_DIM + EPS))
    return (z * inv) * g2


def _dot(a, b):
    return jnp.dot(a, b, preferred_element_type=F32)


def _dot_t(a, b):
    return lax.dot_general(a, b, (((1,), (1,)), ((), ())), preferred_element_type=F32)


def _proj_body(x_ref, gmix_ref, w_ref, gq_ref, gk_ref, lng_ref, lnb_ref,
               q_ref, k_ref, v_ref, u_ref, vs_ref):
    xn = _rms_rows(x_ref[...], gmix_ref[...]).astype(BF16)
    zq = _dot(xn, w_ref[:, 0:C_Q])
    gq = gq_ref[...]
    for c in range(ATT_WIDTH // LANES):
        sl = slice(c * LANES, (c + 1) * LANES)
        q_ref[:, sl] = (_rms_head_pairs(zq[:, sl], gq) * (HEAD_DIM ** -0.5)).astype(q_ref.dtype)
    zkv = _dot(xn, w_ref[:, C_Q:C_V])
    k_ref[...] = _rms_head_pairs(zkv[:, 0:KV_WIDTH], gk_ref[...])
    v_ref[...] = zkv[:, KV_WIDTH:]
    u_ref[...] = jax.nn.gelu(_dot(xn, w_ref[:, C_V:C_U]))
    gv = jax.nn.gelu(_dot(xn, w_ref[:, C_U:]))
    mu = jnp.mean(gv, axis=-1, keepdims=True)
    xc = gv - mu
    y = xc * lax.rsqrt(jnp.mean(xc * xc, axis=-1, keepdims=True) + EPS)
    vs_ref[...] = y * lng_ref[...] + lnb_ref[...]


def _proj(x, gmix, w_in, gq2, gk2, lng, lnb, tm):
    rows, d = x.shape
    in_cols = w_in.shape[1]
    row_spec = lambda width: pl.BlockSpec((tm, width), lambda i: (i, 0))
    full = lambda shape: pl.BlockSpec(shape, lambda i: (0, 0))
    return pl.pallas_call(
        _proj_body,
        grid=(rows // tm,),
        in_specs=[row_spec(d), full((1, d)), full((d, in_cols)), full((1, LANES)), full((1, LANES)),
                  full((1, SGU_WIDTH)), full((1, SGU_WIDTH))],
        out_specs=[row_spec(ATT_WIDTH), row_spec(KV_WIDTH), row_spec(KV_WIDTH),
                   row_spec(SGU_WIDTH), row_spec(SGU_WIDTH)],
        out_shape=[jax.ShapeDtypeStruct((rows, ATT_WIDTH), BF16),
                   jax.ShapeDtypeStruct((rows, KV_WIDTH), F32),
                   jax.ShapeDtypeStruct((rows, KV_WIDTH), F32),
                   jax.ShapeDtypeStruct((rows, SGU_WIDTH), F32),
                   jax.ShapeDtypeStruct((rows, SGU_WIDTH), F32)],
        compiler_params=_cparams(1),
        name="proj",
    )(x, gmix, w_in, gq2, gk2, lng, lnb)


def _alibi_slope(h):
    return 2.0 ** (-8.0 * (h + 1) / ATT_HEADS)


def _softmax_with_sink(s, sink):
    m = jnp.maximum(jnp.max(s, axis=-1, keepdims=True), sink)
    e = jnp.exp(s - m)
    denom = jnp.sum(e, axis=-1, keepdims=True) + jnp.exp(sink - m)
    return e / denom


def _mix_prompt_body(sink_ref, q_ref, kc_ref, kp_ref, vc_ref, vp_ref, u_ref, vs_ref, ws_ref, bs_ref,
                     goa_ref, gos_ref, mix_ref, bias_scr, wtril_scr):
    b = pl.program_id(0)
    n = pl.program_id(1)

    @pl.when((b == 0) & (n == 0))
    def _init():
        row = lax.broadcasted_iota(jnp.int32, (BLOCK, 2 * BLOCK), 0)
        col = lax.broadcasted_iota(jnp.int32, (BLOCK, 2 * BLOCK), 1)
        dist = BLOCK + row - col
        valid = (dist >= 0) & (dist < WINDOW)
        distf = dist.astype(F32)
        for h in range(ATT_HEADS):
            bias = jnp.where(valid, -_alibi_slope(h) * distf, NEG_INF)
            bias_scr[h] = bias
            bias_scr[ATT_HEADS + h] = jnp.where(col < BLOCK, NEG_INF, bias)
        r = lax.broadcasted_iota(jnp.int32, (BLOCK, BLOCK), 0)
        c = lax.broadcasted_iota(jnp.int32, (BLOCK, BLOCK), 1)
        for h in range(SGU_HEADS):
            wtril_scr[h] = jnp.where(r >= c, ws_ref[h], 0.0).astype(BF16)

    bias_base = jnp.where(n == 0, ATT_HEADS, 0)
    lo = lax.broadcasted_iota(jnp.int32, (2 * BLOCK, LANES), 1) < HEAD_DIM
    kband = jnp.concatenate([kp_ref[...], kc_ref[...]], axis=0)
    vband = jnp.concatenate([vp_ref[...], vc_ref[...]], axis=0)
    kroll = pltpu.roll(kband, HEAD_DIM, 1)
    vroll = pltpu.roll(vband, HEAD_DIM, 1)
    q = q_ref[...]
    att_cols = []
    for g in range(KV_HEADS):
        k_src_lo, k_src_hi = (kband, kroll) if g == 0 else (kroll, kband)
        v_src_lo, v_src_hi = (vband, vroll) if g == 0 else (vroll, vband)
        k_par = (jnp.where(lo, k_src_lo, 0.0).astype(BF16), jnp.where(lo, 0.0, k_src_hi).astype(BF16))
        v_par = (jnp.where(lo, v_src_lo, 0.0).astype(BF16), jnp.where(lo, 0.0, v_src_hi).astype(BF16))
        pairs = GROUP // 2
        qp = jnp.concatenate(
            [q[:, (g * pairs + p) * LANES:(g * pairs + p + 1) * LANES] for p in range(pairs)], axis=0)
        o_pair = None
        for par in range(2):
            s_all = _dot_t(qp, k_par[par])
            probs = []
            for p in range(pairs):
                h = g * GROUP + 2 * p + par
                s = s_all[p * BLOCK:(p + 1) * BLOCK, :] + bias_scr[bias_base + h]
                probs.append(_softmax_with_sink(s, sink_ref[h]).astype(BF16))
            o = _dot(jnp.concatenate(probs, axis=0), v_par[par])
            o_pair = o if o_pair is None else o_pair + o
        for p in range(pairs):
            att_cols.append(o_pair[p * BLOCK:(p + 1) * BLOCK, :])
    att = jnp.concatenate(att_cols, axis=-1)
    mix_ref[:, 0:ATT_WIDTH] = _rms_rows(att, goa_ref[...]).astype(mix_ref.dtype)

    vs = vs_ref[...].astype(BF16)
    bs = bs_ref[...]
    sgu_cols = []
    for h in range(SGU_HEADS):
        sl = slice(h * SGU_HEAD_DIM, (h + 1) * SGU_HEAD_DIM)
        mixed = _dot(wtril_scr[h], vs[:, sl]) + bs[:, h:h + 1]
        sgu_cols.append(u_ref[:, sl] * mixed)
    sgu = jnp.concatenate(sgu_cols, axis=-1)
    mix_ref[:, ATT_WIDTH:] = _rms_rows(sgu, gos_ref[...]).astype(mix_ref.dtype)


def _mix_prompt(sinks, q, k, v, u, vs, w_s, b_s_t, goa, gos, batch, seq):
    nb = seq // BLOCK
    cur = lambda width: pl.BlockSpec((BLOCK, width), lambda b, n: (b * nb + n, 0))
    prev = lambda width: pl.BlockSpec((BLOCK, width), lambda b, n: (b * nb + jnp.maximum(n - 1, 0), 0))
    full2 = lambda shape: pl.BlockSpec(shape, lambda b, n: (0, 0))
    return pl.pallas_call(
        _mix_prompt_body,
        grid=(batch, nb),
        in_specs=[pl.BlockSpec(memory_space=pltpu.SMEM),
                  cur(ATT_WIDTH), cur(KV_WIDTH), prev(KV_WIDTH), cur(KV_WIDTH), prev(KV_WIDTH),
                  cur(SGU_WIDTH), cur(SGU_WIDTH),
                  pl.BlockSpec((SGU_HEADS, BLOCK, BLOCK), lambda b, n: (0, 0, 0)),
                  full2((BLOCK, SGU_HEADS)), full2((1, ATT_WIDTH)), full2((1, SGU_WIDTH))],
        out_specs=cur(ATT_WIDTH + SGU_WIDTH),
        out_shape=jax.ShapeDtypeStruct((batch * seq, ATT_WIDTH + SGU_WIDTH), BF16),
        scratch_shapes=[pltpu.VMEM((2 * ATT_HEADS, BLOCK, 2 * BLOCK), F32),
                        pltpu.VMEM((SGU_HEADS, BLOCK, BLOCK), BF16)],
        compiler_params=_cparams(2),
        name="mix_prompt",
    )(sinks, q, k, k, v, v, u, vs, w_s, b_s_t, goa, gos)


def _attn_sample_body(q_ref, kc_ref, kn_ref, vc_ref, vn_ref, sink_ref, o_ref, *, bb, t_new):
    rows = GROUP * t_new
    t_shift = t_new.bit_length() - 1
    assert t_new == 1 << t_shift
    r = lax.broadcasted_iota(jnp.int32, (rows, 1), 0)
    t = jnp.bitwise_and(r, t_new - 1)
    g = lax.shift_right_logical(r, t_shift)
    jc = lax.broadcasted_iota(jnp.int32, (rows, WINDOW), 1)
    jn = lax.broadcasted_iota(jnp.int32, (rows, NEW_PAD), 1)
    dist_c = WINDOW + t - jc
    dist_n = t - jn
    valid_c = (dist_c >= 0) & (dist_c < WINDOW)
    valid_n = (dist_n >= 0) & (jn < t_new)
    lo = lax.broadcasted_iota(jnp.int32, (rows, LANES), 1) < HEAD_DIM
    hi = jnp.logical_not(lo)
    biases = []
    for kv in range(KV_HEADS):
        head = (kv * GROUP + g + 1).astype(F32)
        slope = jnp.exp2(-8.0 * head / ATT_HEADS)
        biases.append((jnp.where(valid_c, -slope * dist_c.astype(F32), NEG_INF),
                       jnp.where(valid_n, -slope * dist_n.astype(F32), NEG_INF)))
    for i in range(bb):
        kc = kc_ref[i].astype(BF16)
        kn = kn_ref[i].astype(BF16)
        vc = vc_ref[i].astype(BF16)
        vn = vn_ref[i].astype(BF16)
        o_kv = []
        for kv in range(KV_HEADS):
            q = jnp.where(lo if kv == 0 else hi, q_ref[i, kv], jnp.zeros((), BF16))
            s_c = _dot_t(q, kc) + biases[kv][0]
            s_n = _dot_t(q, kn) + biases[kv][1]
            sink = sink_ref[kv]
            m = jnp.maximum(jnp.maximum(jnp.max(s_c, axis=-1, keepdims=True),
                                        jnp.max(s_n, axis=-1, keepdims=True)), sink)
            e_c = jnp.exp(s_c - m)
            e_n = jnp.exp(s_n - m)
            denom = (jnp.sum(e_c, axis=-1, keepdims=True) + jnp.sum(e_n, axis=-1, keepdims=True)
                     + jnp.exp(sink - m))
            o_kv.append(_dot((e_c / denom).astype(BF16), vc) + _dot((e_n / denom).astype(BF16), vn))
        o_ref[i] = jnp.where(lo, o_kv[0], o_kv[1])


def _attn_sample(q_ext, k_cache, k_new, v_cache, v_new, sink_rows, t_new, bb=8):
    nbatch = q_ext.shape[0]
    rows = GROUP * t_new
    b3 = lambda shape: pl.BlockSpec((bb,) + shape, lambda i: (i,) + (0,) * len(shape))
    return pl.pallas_call(
        functools.partial(_attn_sample_body, bb=bb, t_new=t_new),
        grid=(nbatch // bb,),
        in_specs=[b3((KV_HEADS, rows, LANES)), b3((WINDOW, KV_WIDTH)), b3((NEW_PAD, KV_WIDTH)),
                  b3((WINDOW, KV_WIDTH)), b3((NEW_PAD, KV_WIDTH)),
                  pl.BlockSpec((KV_HEADS, rows, 1), lambda i: (0, 0, 0))],
        out_specs=b3((rows, LANES)),
        out_shape=jax.ShapeDtypeStruct((nbatch, rows, LANES), F32),
        compiler_params=_cparams(1),
        name="attn_sample",
    )(q_ext, k_cache, k_new, v_cache, v_new, sink_rows)


def _merge_sample_body(att_ref, u_ref, vs_ref, wrow_ref, brow_ref, goa_ref, gos_ref, mix_ref, *, nbatch, t_new):
    mix_ref[:, 0:ATT_WIDTH] = _rms_rows(att_ref[...], goa_ref[...]).astype(mix_ref.dtype)
    for t in range(t_new):
        mixed = None
        for s in range(t + 1):
            term = wrow_ref[t * t_new + s:t * t_new + s + 1, :] * vs_ref[s * nbatch:(s + 1) * nbatch, :]
            mixed = term if mixed is None else mixed + term
        mixed = mixed + brow_ref[t:t + 1, :]
        sgu = u_ref[t * nbatch:(t + 1) * nbatch, :] * mixed
        mix_ref[t * nbatch:(t + 1) * nbatch, ATT_WIDTH:] = _rms_rows(sgu, gos_ref[...]).astype(mix_ref.dtype)


def _merge_sample(att, u, vs, wrow, brow, goa, gos, nbatch, t_new):
    rows = att.shape[0]
    return pl.pallas_call(
        functools.partial(_merge_sample_body, nbatch=nbatch, t_new=t_new),
        out_shape=jax.ShapeDtypeStruct((rows, ATT_WIDTH + SGU_WIDTH), BF16),
        compiler_params=pltpu.CompilerParams(vmem_limit_bytes=VMEM_LIMIT),
        name="merge_sample",
    )(att, u, vs, wrow, brow, goa, gos)


def _out_proj_body(x_ref, mix_ref, w_ref, y_ref):
    y_ref[...] = x_ref[...] + _dot(mix_ref[...], w_ref[...])


def _out_proj(x, mix, w_o, tm):
    rows, d = x.shape
    return pl.pallas_call(
        _out_proj_body,
        grid=(rows // tm,),
        in_specs=[pl.BlockSpec((tm, d), lambda i: (i, 0)),
                  pl.BlockSpec((tm, mix.shape[1]), lambda i: (i, 0)),
                  pl.BlockSpec(w_o.shape, lambda i: (0, 0))],
        out_specs=pl.BlockSpec((tm, d), lambda i: (i, 0)),
        out_shape=jax.ShapeDtypeStruct((rows, d), F32),
        compiler_params=_cparams(1),
        name="out_proj",
    )(x, mix, w_o)


def _silu_gate(hc_a, hc_g):
    return (jax.nn.silu(hc_g) * hc_a).astype(BF16)


def _ffn_prompt_body(x_ref, halo_ref, g_ref, wa_ref, wg_ref, cwa_ref, cwg_ref, cba_ref, cbg_ref, wd_ref,
                     y_ref, sa_ref, sg_ref, xn_scr, ha_scr, hg_scr, *, tm, tiles_per_seq):
    i = pl.program_id(0)
    j = pl.program_id(1)

    @pl.when(j == 0)
    def _prologue():
        x = x_ref[...]
        xn_scr[HALO:, :] = _rms_rows(x, g_ref[...]).astype(BF16)
        keep = jnp.where((i % tiles_per_seq) == 0, 0.0, 1.0)
        xn_scr[0:HALO, :] = (_rms_rows(halo_ref[...], g_ref[...]) * keep).astype(BF16)
        y_ref[...] = x

    xe = xn_scr[...]
    ha_scr[...] = _dot(xe, wa_ref[...])
    hg_scr[...] = _dot(xe, wg_ref[...])

    def conv(h_scr, cw_ref, cb_ref):
        hc = cb_ref[...] + cw_ref[0:1, :] * h_scr[pl.ds(HALO - 2, tm), :]
        hc = hc + cw_ref[1:2, :] * h_scr[pl.ds(HALO - 1, tm), :]
        return hc + cw_ref[2:3, :] * h_scr[pl.ds(HALO, tm), :]

    act = _silu_gate(conv(ha_scr, cwa_ref, cba_ref), conv(hg_scr, cwg_ref, cbg_ref))
    y_ref[...] += _dot(act, wd_ref[...])
    sa_ref[0] = ha_scr[pl.ds(HALO + tm - (CONV_W - 1), CONV_W - 1), :]
    sg_ref[0] = hg_scr[pl.ds(HALO + tm - (CONV_W - 1), CONV_W - 1), :]


def _ffn_prompt(x, g, w_up, cw, cb, w_down, batch, seq, tm, tf):
    rows, d = x.shape
    ffp = w_down.shape[0]
    nj = ffp // tf
    tiles_per_seq = seq // tm
    halo_blocks = tm // HALO
    return pl.pallas_call(
        functools.partial(_ffn_prompt_body, tm=tm, tiles_per_seq=tiles_per_seq),
        grid=(rows // tm, nj),
        in_specs=[pl.BlockSpec((tm, d), lambda i, j: (i, 0)),
                  pl.BlockSpec((HALO, d), lambda i, j: (jnp.maximum(i * halo_blocks - 1, 0), 0)),
                  pl.BlockSpec((1, d), lambda i, j: (0, 0)),
                  pl.BlockSpec((d, tf), lambda i, j: (0, j)),
                  pl.BlockSpec((d, tf), lambda i, j: (0, j + nj)),
                  pl.BlockSpec((CONV_W, tf), lambda i, j: (0, j)),
                  pl.BlockSpec((CONV_W, tf), lambda i, j: (0, j + nj)),
                  pl.BlockSpec((1, tf), lambda i, j: (0, j)),
                  pl.BlockSpec((1, tf), lambda i, j: (0, j + nj)),
                  pl.BlockSpec((tf, d), lambda i, j: (j, 0))],
        out_specs=[pl.BlockSpec((tm, d), lambda i, j: (i, 0)),
                   pl.BlockSpec((1, CONV_W - 1, tf), lambda i, j: (i, 0, j)),
                   pl.BlockSpec((1, CONV_W - 1, tf), lambda i, j: (i, 0, j))],
        out_shape=[jax.ShapeDtypeStruct((rows, d), F32),
                   jax.ShapeDtypeStruct((rows // tm, CONV_W - 1, ffp), F32),
                   jax.ShapeDtypeStruct((rows // tm, CONV_W - 1, ffp), F32)],
        scratch_shapes=[pltpu.VMEM((HALO + tm, d), BF16),
                        pltpu.VMEM((HALO + tm, tf), F32),
                        pltpu.VMEM((HALO + tm, tf), F32)],
        compiler_params=_cparams(2),
        name="ffn_prompt",
    )(x, x, g, w_up, w_up, cw, cw, cb, cb, w_down)


def _ffn_sample_body(x_ref, g_ref, wa_ref, wg_ref, cwa_ref, cwg_ref, cba_ref, cbg_ref, wd_ref, bufa_ref, bufg_ref,
                     y_ref, sa_ref, sg_ref, xn_scr, *, nbatch, t_new):
    j = pl.program_id(0)

    @pl.when(j == 0)
    def _prologue():
        x = x_ref[...]
        xn_scr[...] = _rms_rows(x, g_ref[...]).astype(BF16)
        y_ref[...] = x

    xn = xn_scr[...]
    ha = _dot(xn, wa_ref[...])
    hg = _dot(xn, wg_ref[...])

    def conv(h, buf_ref, cw_ref, cb_ref):
        hp = [buf_ref[k] for k in range(CONV_W - 1)] + [h[t * nbatch:(t + 1) * nbatch, :] for t in range(t_new)]
        out = []
        for t in range(t_new):
            hc = cb_ref[...] + cw_ref[0:1, :] * hp[t]
            for k in range(1, CONV_W):
                hc = hc + cw_ref[k:k + 1, :] * hp[t + k]
            out.append(hc)
        return jnp.concatenate(out, axis=0)

    act = _silu_gate(conv(ha, bufa_ref, cwa_ref, cba_ref), conv(hg, bufg_ref, cwg_ref, cbg_ref))
    y_ref[...] += _dot(act, wd_ref[...])
    for k in range(CONV_W - 1):
        t = t_new - (CONV_W - 1) + k
        sa_ref[k] = ha[t * nbatch:(t + 1) * nbatch, :]
        sg_ref[k] = hg[t * nbatch:(t + 1) * nbatch, :]


def _ffn_sample(x, g, w_up, cw, cb, w_down, buf, nbatch, t_new, tf):
    rows, d = x.shape
    ffp = w_down.shape[0]
    nj = ffp // tf
    state = lambda off: pl.BlockSpec((CONV_W - 1, nbatch, tf), lambda j: (0, 0, j + off))
    return pl.pallas_call(
        functools.partial(_ffn_sample_body, nbatch=nbatch, t_new=t_new),
        grid=(nj,),
        in_specs=[pl.BlockSpec((rows, d), lambda j: (0, 0)),
                  pl.BlockSpec((1, d), lambda j: (0, 0)),
                  pl.BlockSpec((d, tf), lambda j: (0, j)),
                  pl.BlockSpec((d, tf), lambda j: (0, j + nj)),
                  pl.BlockSpec((CONV_W, tf), lambda j: (0, j)),
                  pl.BlockSpec((CONV_W, tf), lambda j: (0, j + nj)),
                  pl.BlockSpec((1, tf), lambda j: (0, j)),
                  pl.BlockSpec((1, tf), lambda j: (0, j + nj)),
                  pl.BlockSpec((tf, d), lambda j: (j, 0)),
                  state(0), state(nj)],
        out_specs=[pl.BlockSpec((rows, d), lambda j: (0, 0)), state(0), state(0)],
        out_shape=[jax.ShapeDtypeStruct((rows, d), F32),
                   jax.ShapeDtypeStruct((CONV_W - 1, nbatch, ffp), F32),
                   jax.ShapeDtypeStruct((CONV_W - 1, nbatch, ffp), F32)],
        scratch_shapes=[pltpu.VMEM((rows, d), BF16)],
        compiler_params=_cparams(1),
        name="ffn_sample",
    )(x, g, w_up, w_up, cw, cw, cb, cb, w_down, buf, buf)


def _pad_ff_halves(a, ffp):
    d_ff = a.shape[-1] // 2
    pad = [(0, 0)] * (a.ndim - 1) + [(0, ffp - d_ff)]
    return jnp.concatenate([jnp.pad(a[..., :d_ff], pad), jnp.pad(a[..., d_ff:], pad)], axis=-1)


def _unpad_ff_halves(sa, sg, d_ff):
    return jnp.concatenate([sa[..., :d_ff], sg[..., :d_ff]], axis=-1)


def kernel(x_prompt, x_sample, cache_k_win, cache_v_win, state_ffn_conv, norm_mix_g, w_in, q_norm_g, k_norm_g,
           attn_sinks, sgu_ln_g, sgu_ln_b, sgu_w, sgu_b, out_norm_att_g, out_norm_sgu_g, w_o, norm_ffn_g,
           w_up, conv_w, conv_b, w_down):
    batch, seq, d = x_prompt.shape
    nbatch, t_new, _ = x_sample.shape
    depth = w_in.shape[0]
    d_ff = w_down.shape[1]
    tf = 512
    ffp = -(-d_ff // tf) * tf

    w_in_b = w_in.astype(BF16)
    w_o_b = w_o.astype(BF16)
    w_up_b = _pad_ff_halves(w_up, ffp).astype(BF16)
    w_down_b = jnp.pad(w_down, ((0, 0), (0, ffp - d_ff), (0, 0))).astype(BF16)
    cw_p = _pad_ff_halves(conv_w, ffp)
    cb_p = _pad_ff_halves(conv_b, ffp)[:, None, :]
    gq2 = jnp.tile(q_norm_g, (1, 2))[:, None, :]
    gk2 = jnp.tile(k_norm_g, (1, 2))[:, None, :]
    b_s_t = jnp.swapaxes(sgu_b, 1, 2)
    wrow = jnp.repeat(jnp.transpose(sgu_w[:, :, :t_new, :t_new], (0, 2, 3, 1)), SGU_HEAD_DIM, axis=-1)
    wrow = wrow.reshape(depth, t_new * t_new, SGU_WIDTH)
    brow = jnp.repeat(jnp.transpose(sgu_b[:, :, :t_new], (0, 2, 1)), SGU_HEAD_DIM, axis=-1)
    sink_rows = jnp.repeat(attn_sinks.reshape(depth, KV_HEADS, GROUP), t_new, axis=-1)[..., None]
    buf_tm = _pad_ff_halves(jnp.swapaxes(state_ffn_conv, 1, 2), ffp)

    xp = x_prompt.reshape(batch * seq, d)
    xs = jnp.swapaxes(x_sample, 0, 1).reshape(t_new * nbatch, d)
    kp_l, vp_l, cp_l, ks_l, vs_l, cs_l, sv_l = [], [], [], [], [], [], []
    for l in range(depth):
        gmix = norm_mix_g[l][None]
        lng, lnb = sgu_ln_g[l][None], sgu_ln_b[l][None]
        goa, gos = out_norm_att_g[l][None], out_norm_sgu_g[l][None]
        gffn = norm_ffn_g[l][None]

        q, k, v, u, vs = _proj(xp, gmix, w_in_b[l], gq2[l], gk2[l], lng, lnb, tm=256)
        mix = _mix_prompt(attn_sinks[l], q, k, v, u, vs, sgu_w[l], b_s_t[l], goa, gos, batch, seq)
        xp = _out_proj(xp, mix, w_o_b[l], tm=512)
        xp, sa, sg = _ffn_prompt(xp, gffn, w_up_b[l], cw_p[l], cb_p[l], w_down_b[l], batch, seq, tm=512, tf=tf)
        kp_l.append(k.reshape(batch, seq, KV_HEADS, HEAD_DIM)[:, -WINDOW:])
        vp_l.append(v.reshape(batch, seq, KV_HEADS, HEAD_DIM)[:, -WINDOW:])
        tiles_per_seq = sa.shape[0] // batch
        cp_l.append(_unpad_ff_halves(sa, sg, d_ff)[tiles_per_seq - 1::tiles_per_seq])

        q, k, v, u, vs = _proj(xs, gmix, w_in_b[l], gq2[l], gk2[l], lng, lnb, tm=256)
        k_bm = jnp.swapaxes(k.reshape(t_new, nbatch, KV_WIDTH), 0, 1)
        v_bm = jnp.swapaxes(v.reshape(t_new, nbatch, KV_WIDTH), 0, 1)
        pad_new = ((0, 0), (0, NEW_PAD - t_new), (0, 0))
        q5 = q.reshape(t_new, nbatch, KV_HEADS, GROUP, HEAD_DIM)
        q5 = jnp.transpose(q5, (1, 2, 3, 0, 4)).reshape(nbatch, KV_HEADS, GROUP * t_new, HEAD_DIM)
        q_ext = jnp.concatenate([q5, q5], axis=-1)
        k_cache = cache_k_win[l].reshape(nbatch, WINDOW, KV_WIDTH)
        v_cache = cache_v_win[l].reshape(nbatch, WINDOW, KV_WIDTH)
        o_ext = _attn_sample(q_ext, k_cache, jnp.pad(k_bm, pad_new), v_cache, jnp.pad(v_bm, pad_new),
                             sink_rows[l], t_new)
        o5 = o_ext.reshape(nbatch, GROUP, t_new, KV_HEADS, HEAD_DIM)
        att = jnp.transpose(o5, (2, 0, 3, 1, 4)).reshape(t_new * nbatch, ATT_WIDTH)
        mix = _merge_sample(att, u, vs, wrow[l], brow[l], goa, gos, nbatch, t_new)
        xs = _out_proj(xs, mix, w_o_b[l], tm=t_new * nbatch)
        xs, sa, sg = _ffn_sample(xs, gffn, w_up_b[l], cw_p[l], cb_p[l], w_down_b[l], buf_tm[l], nbatch, t_new, tf=tf)
        ks_l.append(jnp.concatenate([cache_k_win[l][:, t_new:],
                                     k_bm.reshape(nbatch, t_new, KV_HEADS, HEAD_DIM)], axis=1))
        vs_l.append(jnp.concatenate([cache_v_win[l][:, t_new:],
                                     v_bm.reshape(nbatch, t_new, KV_HEADS, HEAD_DIM)], axis=1))
        cs_l.append(jnp.swapaxes(_unpad_ff_halves(sa, sg, d_ff), 0, 1))
        sv_l.append(jnp.swapaxes(vs.reshape(t_new, nbatch, SGU_HEADS, SGU_HEAD_DIM), 0, 1))

    y_prompt = xp.reshape(batch, seq, d)
    y_sample = jnp.swapaxes(xs.reshape(t_new, nbatch, d), 0, 1)
    return (y_prompt, y_sample, jnp.stack(kp_l), jnp.stack(vp_l), jnp.stack(cp_l),
            jnp.stack(ks_l), jnp.stack(vs_l), jnp.stack(cs_l), jnp.stack(sv_l))
```

```python
import functools

import jax
import jax.numpy as jnp
from jax import lax
from jax.experimental import pallas as pl
from jax.experimental.pallas import tpu as pltpu

F32 = jnp.float32
BF16 = jnp.bfloat16

HEAD_DIM = 64
ATT_HEADS = 16
KV_HEADS = 2
GROUP = ATT_HEADS // KV_HEADS
ATT_WIDTH = ATT_HEADS * HEAD_DIM
KV_WIDTH = KV_HEADS * HEAD_DIM
SGU_HEADS = 8
SGU_HEAD_DIM = 128
SGU_WIDTH = SGU_HEADS * SGU_HEAD_DIM
WINDOW = 128
BLOCK = 128
CONV_W = 3
EPS = 1e-6
NEG_INF = -1e30
LANES = 128
MXU_WIDTH = 256
HALO = 16
NEW_PAD = 16
VMEM_LIMIT = 56 * 1024 * 1024

TM_PROJ = 256
TM_OUT = 512
TM_FFN = 512
TF_FFN = 512
BB_ATTN = 16

C_Q = ATT_WIDTH
C_K = C_Q + KV_WIDTH
C_V = C_K + KV_WIDTH
C_U = C_V + SGU_WIDTH


def _cparams(n_axes, flags=None):
    return pltpu.CompilerParams(dimension_semantics=("arbitrary",) * n_axes,
                                vmem_limit_bytes=VMEM_LIMIT, flags=flags)


def _layer_spec(l, shape):
    zeros = (0,) * len(shape)
    return pl.BlockSpec((None,) + tuple(shape), lambda *_: (l,) + zeros)


def _rms_rows(x, g):
    y = x * lax.rsqrt(jnp.mean(x * x, axis=-1, keepdims=True) + EPS)
    return y * g


def _rms_head_pairs(z, g2):
    lo = lax.broadcasted_iota(jnp.int32, z.shape, 1) < HEAD_DIM
    z2 = z * z
    s_lo = jnp.sum(jnp.where(lo, z2, 0.0), axis=-1, keepdims=True)
    s_hi = jnp.sum(jnp.where(lo, 0.0, z2), axis=-1, keepdims=True)
    inv = jnp.where(lo, lax.rsqrt(s_lo / HEAD_DIM + EPS), lax.rsqrt(s_hi / HEAD_DIM + EPS))
    return (z * inv) * g2


def _dot(a, b):
    return jnp.dot(a, b, preferred_element_type=F32)


def _dot_t(a, b):
    return lax.dot_general(a, b, (((1,), (1,)), ((), ())), preferred_element_type=F32)


def _proj_body(x_ref, gmix_ref, w_ref, gq_ref, gk_ref, lng_ref, lnb_ref,
               q_ref, k_ref, v_ref, u_ref, vs_ref):
    xn = _rms_rows(x_ref[...], gmix_ref[...]).astype(BF16)
    zq = _dot(xn, w_ref[:, 0:C_Q])
    gq = gq_ref[...]
    for c in range(ATT_WIDTH // LANES):
        sl = slice(c * LANES, (c + 1) * LANES)
        q_ref[:, sl] = (_rms_head_pairs(zq[:, sl], gq) * (HEAD_DIM ** -0.5)).astype(q_ref.dtype)
    zkv = _dot(xn, w_ref[:, C_Q:C_V])
    k_ref[...] = _rms_head_pairs(zkv[:, 0:KV_WIDTH], gk_ref[...])
    v_ref[...] = zkv[:, KV_WIDTH:]
    u_ref[...] = jax.nn.gelu(_dot(xn, w_ref[:, C_V:C_U]))
    gv = jax.nn.gelu(_dot(xn, w_ref[:, C_U:]))
    mu = jnp.mean(gv, axis=-1, keepdims=True)
    xc = gv - mu
    y = xc * lax.rsqrt(jnp.mean(xc * xc, axis=-1, keepdims=True) + EPS)
    vs_ref[...] = y * lng_ref[...] + lnb_ref[...]


def _proj(l, x, gmix, w_in, gq2, gk2, lng, lnb):
    rows, d = x.shape
    tm = min(TM_PROJ, rows)
    in_cols = w_in.shape[-1]
    row_spec = lambda width: pl.BlockSpec((tm, width), lambda i: (i, 0))
    return pl.pallas_call(
        _proj_body,
        grid=(rows // tm,),
        in_specs=[row_spec(d), _layer_spec(l, (1, d)), _layer_spec(l, (d, in_cols)),
                  _layer_spec(l, (1, LANES)), _layer_spec(l, (1, LANES)),
                  _layer_spec(l, (1, SGU_WIDTH)), _layer_spec(l, (1, SGU_WIDTH))],
        out_specs=[row_spec(ATT_WIDTH), row_spec(KV_WIDTH), row_spec(KV_WIDTH),
                   row_spec(SGU_WIDTH), row_spec(SGU_WIDTH)],
        out_shape=[jax.ShapeDtypeStruct((rows, ATT_WIDTH), BF16),
                   jax.ShapeDtypeStruct((rows, KV_WIDTH), F32),
                   jax.ShapeDtypeStruct((rows, KV_WIDTH), F32),
                   jax.ShapeDtypeStruct((rows, SGU_WIDTH), F32),
                   jax.ShapeDtypeStruct((rows, SGU_WIDTH), F32)],
        compiler_params=_cparams(1),
        name="proj",
    )(x, gmix, w_in, gq2, gk2, lng, lnb)


def _alibi_slope(h):
    return 2.0 ** (-8.0 * (h + 1) / ATT_HEADS)


def _softmax_with_sink(s, sink):
    m = jnp.maximum(jnp.max(s, axis=-1, keepdims=True), sink)
    e = jnp.exp(s - m)
    denom = jnp.sum(e, axis=-1, keepdims=True) + jnp.exp(sink - m)
    return e / denom


def _mix_prompt_body(sink_ref, q_ref, kc_ref, kp_ref, vc_ref, vp_ref, u_ref, vs_ref, ws_ref, bs_ref,
                     goa_ref, gos_ref, mix_ref, bias_scr, wtril_scr, *, l):
    b = pl.program_id(0)
    n = pl.program_id(1)

    @pl.when((b == 0) & (n == 0))
    def _init():
        row = lax.broadcasted_iota(jnp.int32, (BLOCK, 2 * BLOCK), 0)
        col = lax.broadcasted_iota(jnp.int32, (BLOCK, 2 * BLOCK), 1)
        dist = BLOCK + row - col
        valid = (dist >= 0) & (dist < WINDOW)
        distf = dist.astype(F32)
        for h in range(ATT_HEADS):
            bias = jnp.where(valid, -_alibi_slope(h) * distf, NEG_INF)
            bias_scr[h] = bias
            bias_scr[ATT_HEADS + h] = jnp.where(col < BLOCK, NEG_INF, bias)
        r = lax.broadcasted_iota(jnp.int32, (BLOCK, BLOCK), 0)
        c = lax.broadcasted_iota(jnp.int32, (BLOCK, BLOCK), 1)
        for h in range(SGU_HEADS):
            wtril_scr[h] = jnp.where(r >= c, ws_ref[h], 0.0).astype(BF16)

    bias_base = jnp.where(n == 0, ATT_HEADS, 0)
    lo = lax.broadcasted_iota(jnp.int32, (2 * BLOCK, LANES), 1) < HEAD_DIM
    kband = jnp.concatenate([kp_ref[...], kc_ref[...]], axis=0)
    vband = jnp.concatenate([vp_ref[...], vc_ref[...]], axis=0)
    kroll = pltpu.roll(kband, HEAD_DIM, 1)
    vroll = pltpu.roll(vband, HEAD_DIM, 1)
    q = q_ref[...]
    att_cols = []
    for g in range(KV_HEADS):
        k_src_lo, k_src_hi = (kband, kroll) if g == 0 else (kroll, kband)
        v_src_lo, v_src_hi = (vband, vroll) if g == 0 else (vroll, vband)
        k_par = (jnp.where(lo, k_src_lo, 0.0).astype(BF16), jnp.where(lo, 0.0, k_src_hi).astype(BF16))
        v_par = (jnp.where(lo, v_src_lo, 0.0).astype(BF16), jnp.where(lo, 0.0, v_src_hi).astype(BF16))
        pairs = GROUP // 2
        qp = jnp.concatenate(
            [q[:, (g * pairs + p) * LANES:(g * pairs + p + 1) * LANES] for p in range(pairs)], axis=0)
        o_pair = None
        for par in range(2):
            s_all = _dot_t(qp, k_par[par])
            probs = []
            for p in range(pairs):
                h = g * GROUP + 2 * p + par
                s = s_all[p * BLOCK:(p + 1) * BLOCK, :] + bias_scr[bias_base + h]
                probs.append(_softmax_with_sink(s, sink_ref[l, h]).astype(BF16))
            o = _dot(jnp.concatenate(probs, axis=0), v_par[par])
            o_pair = o if o_pair is None else o_pair + o
        for p in range(pairs):
            att_cols.append(o_pair[p * BLOCK:(p + 1) * BLOCK, :])
    att = jnp.concatenate(att_cols, axis=-1)
    mix_ref[:, 0:ATT_WIDTH] = _rms_rows(att, goa_ref[...]).astype(mix_ref.dtype)

    vs = vs_ref[...].astype(BF16)
    bs = bs_ref[...]
    sgu_cols = []
    for h in range(SGU_HEADS):
        sl = slice(h * SGU_HEAD_DIM, (h + 1) * SGU_HEAD_DIM)
        mixed = _dot(wtril_scr[h], vs[:, sl]) + bs[:, h:h + 1]
        sgu_cols.append(u_ref[:, sl] * mixed)
    sgu = jnp.concatenate(sgu_cols, axis=-1)
    mix_ref[:, ATT_WIDTH:] = _rms_rows(sgu, gos_ref[...]).astype(mix_ref.dtype)


def _mix_prompt(l, sinks, q, k, v, u, vs, w_s, b_s_t, goa, gos, batch, seq):
    nb = seq // BLOCK
    cur = lambda width: pl.BlockSpec((BLOCK, width), lambda b, n: (b * nb + n, 0))
    prev = lambda width: pl.BlockSpec((BLOCK, width), lambda b, n: (b * nb + jnp.maximum(n - 1, 0), 0))
    return pl.pallas_call(
        functools.partial(_mix_prompt_body, l=l),
        grid=(batch, nb),
        in_specs=[pl.BlockSpec(memory_space=pltpu.SMEM),
                  cur(ATT_WIDTH), cur(KV_WIDTH), prev(KV_WIDTH), cur(KV_WIDTH), prev(KV_WIDTH),
                  cur(SGU_WIDTH), cur(SGU_WIDTH),
                  _layer_spec(l, (SGU_HEADS, BLOCK, BLOCK)), _layer_spec(l, (BLOCK, SGU_HEADS)),
                  _layer_spec(l, (1, ATT_WIDTH)), _layer_spec(l, (1, SGU_WIDTH))],
        out_specs=cur(ATT_WIDTH + SGU_WIDTH),
        out_shape=jax.ShapeDtypeStruct((batch * seq, ATT_WIDTH + SGU_WIDTH), BF16),
        scratch_shapes=[pltpu.VMEM((2 * ATT_HEADS, BLOCK, 2 * BLOCK), F32),
                        pltpu.VMEM((SGU_HEADS, BLOCK, BLOCK), BF16)],
        compiler_params=_cparams(2),
        name="mix_prompt",
    )(sinks, q, k, k, v, v, u, vs, w_s, b_s_t, goa, gos)


def _attn_sample_body(q_ref, kc_ref, kn_ref, vc_ref, vn_ref, sink_ref, o_ref, kw_ref, vw_ref, *, t_new):
    rows = KV_HEADS * GROUP * t_new
    keys = WINDOW + NEW_PAD
    t_shift = t_new.bit_length() - 1
    assert t_new == 1 << t_shift
    r = lax.broadcasted_iota(jnp.int32, (rows, 1), 0)
    t = jnp.bitwise_and(r, t_new - 1)
    head = lax.shift_right_logical(r, t_shift)
    j = lax.broadcasted_iota(jnp.int32, (rows, keys), 1)
    dist = WINDOW + t - j
    valid = (dist >= 0) & (dist < WINDOW)
    slope = jnp.exp2(-8.0 * (head + 1).astype(F32) / ATT_HEADS)
    bias = jnp.where(valid, -slope * dist.astype(F32), NEG_INF)
    lane_lo = lax.broadcasted_iota(jnp.int32, (rows, LANES), 1) < HEAD_DIM
    row_kv0 = lax.broadcasted_iota(jnp.int32, (rows, LANES), 0) < GROUP * t_new
    q_keep = jnp.where(lane_lo == row_kv0, 1.0, 0.0).astype(BF16)

    kx = jnp.concatenate([kc_ref[...], kn_ref[...]], axis=1)
    vx = jnp.concatenate([vc_ref[...], vn_ref[...]], axis=1)
    for w_ref, c_ref, n_ref in ((kw_ref, kc_ref, kn_ref), (vw_ref, vc_ref, vn_ref)):
        w_ref[:, 0:WINDOW - t_new, :] = c_ref[:, t_new:WINDOW, :]
        w_ref[:, WINDOW - t_new:WINDOW, :] = n_ref[:, 0:t_new, :]
    q = q_ref[...] * q_keep[None]
    s = jnp.einsum('brd,bjd->brj', q, kx.astype(BF16), preferred_element_type=F32) + bias[None]
    sink = sink_ref[...][None]
    m = jnp.maximum(jnp.max(s, axis=-1, keepdims=True), sink)
    e = jnp.exp(s - m)
    denom = jnp.sum(e, axis=-1, keepdims=True) + jnp.exp(sink - m)
    p = (e / denom).astype(BF16)
    o = jnp.einsum('brj,bjd->brd', p, vx.astype(BF16), preferred_element_type=F32)
    half = GROUP * t_new
    out_lo = lax.broadcasted_iota(jnp.int32, (o.shape[0], half, LANES), 2) < HEAD_DIM
    o_ref[...] = jnp.where(out_lo, o[:, :half], o[:, half:])


def _attn_sample(l, q_ext, k_cache, k_new, v_cache, v_new, sink_rows, t_new):
    nbatch = q_ext.shape[0]
    bb = BB_ATTN
    rows = KV_HEADS * GROUP * t_new
    b3 = lambda shape: pl.BlockSpec((bb,) + shape, lambda i: (i,) + (0,) * len(shape))
    cache = pl.BlockSpec((None, bb, WINDOW, KV_WIDTH), lambda i: (l, i, 0, 0))
    return pl.pallas_call(
        functools.partial(_attn_sample_body, t_new=t_new),
        grid=(nbatch // bb,),
        in_specs=[b3((rows, LANES)), cache, b3((NEW_PAD, KV_WIDTH)), cache, b3((NEW_PAD, KV_WIDTH)),
                  _layer_spec(l, (rows, 1))],
        out_specs=[b3((rows // KV_HEADS, LANES)), b3((WINDOW, KV_WIDTH)), b3((WINDOW, KV_WIDTH))],
        out_shape=[jax.ShapeDtypeStruct((nbatch, rows // KV_HEADS, LANES), F32),
                   jax.ShapeDtypeStruct((nbatch, WINDOW, KV_WIDTH), F32),
                   jax.ShapeDtypeStruct((nbatch, WINDOW, KV_WIDTH), F32)],
        compiler_params=_cparams(1),
        name="attn_sample",
    )(q_ext, k_cache, k_new, v_cache, v_new, sink_rows)


def _merge_sample_body(att_ref, u_ref, vs_ref, wrow_ref, brow_ref, goa_ref, gos_ref, mix_ref, *, nbatch, t_new):
    mix_ref[:, 0:ATT_WIDTH] = _rms_rows(att_ref[...], goa_ref[...]).astype(mix_ref.dtype)
    for t in range(t_new):
        mixed = None
        for s in range(t + 1):
            term = wrow_ref[t * t_new + s:t * t_new + s + 1, :] * vs_ref[s * nbatch:(s + 1) * nbatch, :]
            mixed = term if mixed is None else mixed + term
        mixed = mixed + brow_ref[t:t + 1, :]
        sgu = u_ref[t * nbatch:(t + 1) * nbatch, :] * mixed
        mix_ref[t * nbatch:(t + 1) * nbatch, ATT_WIDTH:] = _rms_rows(sgu, gos_ref[...]).astype(mix_ref.dtype)


def _merge_sample(l, att, u, vs, wrow, brow, goa, gos, nbatch, t_new):
    rows = att.shape[0]
    whole = lambda a: pl.BlockSpec(a.shape, lambda i: (0,) * a.ndim)
    return pl.pallas_call(
        functools.partial(_merge_sample_body, nbatch=nbatch, t_new=t_new),
        grid=(1,),
        in_specs=[whole(att), whole(u), whole(vs), _layer_spec(l, wrow.shape[1:]), _layer_spec(l, brow.shape[1:]),
                  _layer_spec(l, (1, ATT_WIDTH)), _layer_spec(l, (1, SGU_WIDTH))],
        out_specs=pl.BlockSpec((rows, ATT_WIDTH + SGU_WIDTH), lambda i: (0, 0)),
        out_shape=jax.ShapeDtypeStruct((rows, ATT_WIDTH + SGU_WIDTH), BF16),
        compiler_params=_cparams(1),
        name="merge_sample",
    )(att, u, vs, wrow, brow, goa, gos)


def _out_proj_body(x_ref, mix_ref, w_ref, y_ref):
    y_ref[...] = x_ref[...] + _dot(mix_ref[...], w_ref[...])


def _out_proj(l, x, mix, w_o):
    rows, d = x.shape
    tm = min(TM_OUT, rows)
    return pl.pallas_call(
        _out_proj_body,
        grid=(rows // tm,),
        in_specs=[pl.BlockSpec((tm, d), lambda i: (i, 0)),
                  pl.BlockSpec((tm, mix.shape[1]), lambda i: (i, 0)),
                  _layer_spec(l, w_o.shape[1:])],
        out_specs=pl.BlockSpec((tm, d), lambda i: (i, 0)),
        out_shape=jax.ShapeDtypeStruct((rows, d), F32),
        compiler_params=_cparams(1),
        name="out_proj",
    )(x, mix, w_o)


def _silu_gate(hc_a, hc_g):
    return (jax.nn.silu(hc_g) * hc_a).astype(BF16)


def _ffn_prompt_body(x_ref, halo_ref, g_ref, wa_ref, wg_ref, cwa_ref, cwg_ref, cba_ref, cbg_ref, wd_ref,
                     y_ref, sa_ref, sg_ref, xn_scr, ha_scr, hg_scr, *, tm, tf, tiles_per_seq):
    i = pl.program_id(0)
    j = pl.program_id(1)

    @pl.when(j == 0)
    def _prologue():
        x = x_ref[...]
        xn_scr[HALO:, :] = _rms_rows(x, g_ref[...]).astype(BF16)
        keep = jnp.where((i % tiles_per_seq) == 0, 0.0, 1.0)
        xn_scr[0:HALO, :] = (_rms_rows(halo_ref[...], g_ref[...]) * keep).astype(BF16)
        y_ref[...] = x

    xe = xn_scr[...]

    def conv(h_scr, cw_ref, cb_ref, cs):
        hc = cb_ref[:, cs] + cw_ref[0:1, cs] * h_scr[pl.ds(HALO - 2, tm), cs]
        hc = hc + cw_ref[1:2, cs] * h_scr[pl.ds(HALO - 1, tm), cs]
        return hc + cw_ref[2:3, cs] * h_scr[pl.ds(HALO, tm), cs]

    for c in range(tf // MXU_WIDTH):
        cs = slice(c * MXU_WIDTH, (c + 1) * MXU_WIDTH)
        ha_scr[:, cs] = _dot(xe, wa_ref[:, cs])
        hg_scr[:, cs] = _dot(xe, wg_ref[:, cs])
    for c in range(tf // MXU_WIDTH):
        cs = slice(c * MXU_WIDTH, (c + 1) * MXU_WIDTH)
        act = _silu_gate(conv(ha_scr, cwa_ref, cba_ref, cs), conv(hg_scr, cwg_ref, cbg_ref, cs))
        y_ref[...] += _dot(act, wd_ref[cs, :])
    sa_ref[...] = ha_scr[pl.ds(HALO + tm - (CONV_W - 1), CONV_W - 1), :]
    sg_ref[...] = hg_scr[pl.ds(HALO + tm - (CONV_W - 1), CONV_W - 1), :]


def _ffn_prompt(l, x, g, w_up, cw, cb, w_down, seq):
    rows, d = x.shape
    tm, tf = TM_FFN, TF_FFN
    ffp = w_down.shape[1]
    nj = ffp // tf
    tiles_per_seq = seq // tm
    halo_blocks = tm // HALO
    lyr = lambda shape, imap: pl.BlockSpec((None,) + shape, lambda i, j: (l,) + imap(i, j))
    return pl.pallas_call(
        functools.partial(_ffn_prompt_body, tm=tm, tf=tf, tiles_per_seq=tiles_per_seq),
        grid=(rows // tm, nj),
        in_specs=[pl.BlockSpec((tm, d), lambda i, j: (i, 0)),
                  pl.BlockSpec((HALO, d), lambda i, j: (jnp.maximum(i * halo_blocks - 1, 0), 0)),
                  _layer_spec(l, (1, d)),
                  lyr((d, tf), lambda i, j: (0, j)),
                  lyr((d, tf), lambda i, j: (0, j + nj)),
                  lyr((CONV_W, tf), lambda i, j: (0, j)),
                  lyr((CONV_W, tf), lambda i, j: (0, j + nj)),
                  lyr((1, tf), lambda i, j: (0, j)),
                  lyr((1, tf), lambda i, j: (0, j + nj)),
                  lyr((tf, d), lambda i, j: (j, 0))],
        out_specs=[pl.BlockSpec((tm, d), lambda i, j: (i, 0)),
                   pl.BlockSpec((None, CONV_W - 1, tf), lambda i, j: (i, 0, j)),
                   pl.BlockSpec((None, CONV_W - 1, tf), lambda i, j: (i, 0, j))],
        out_shape=[jax.ShapeDtypeStruct((rows, d), F32),
                   jax.ShapeDtypeStruct((rows // tm, CONV_W - 1, ffp), F32),
                   jax.ShapeDtypeStruct((rows // tm, CONV_W - 1, ffp), F32)],
        scratch_shapes=[pltpu.VMEM((HALO + tm, d), BF16),
                        pltpu.VMEM((HALO + tm, tf), F32),
                        pltpu.VMEM((HALO + tm, tf), F32)],
        compiler_params=_cparams(2),
        name="ffn_prompt",
    )(x, x, g, w_up, w_up, cw, cw, cb, cb, w_down)


def _ffn_sample_body(x_ref, g_ref, wa_ref, wg_ref, cwa_ref, cwg_ref, cba_ref, cbg_ref, wd_ref, bufa_ref, bufg_ref,
                     y_ref, sa_ref, sg_ref, xn_scr, *, nbatch, t_new):
    j = pl.program_id(0)

    @pl.when(j == 0)
    def _prologue():
        x = x_ref[...]
        xn_scr[...] = _rms_rows(x, g_ref[...]).astype(BF16)
        y_ref[...] = x

    xn = xn_scr[...]
    ha = _dot(xn, wa_ref[...])
    hg = _dot(xn, wg_ref[...])

    def conv(h, buf_ref, cw_ref, cb_ref):
        hp = [buf_ref[k] for k in range(CONV_W - 1)] + [h[t * nbatch:(t + 1) * nbatch, :] for t in range(t_new)]
        out = []
        for t in range(t_new):
            hc = cb_ref[...] + cw_ref[0:1, :] * hp[t]
            for k in range(1, CONV_W):
                hc = hc + cw_ref[k:k + 1, :] * hp[t + k]
            out.append(hc)
        return jnp.concatenate(out, axis=0)

    act = _silu_gate(conv(ha, bufa_ref, cwa_ref, cba_ref), conv(hg, bufg_ref, cwg_ref, cbg_ref))
    y_ref[...] += _dot(act, wd_ref[...])
    for k in range(CONV_W - 1):
        t = t_new - (CONV_W - 1) + k
        sa_ref[k] = ha[t * nbatch:(t + 1) * nbatch, :]
        sg_ref[k] = hg[t * nbatch:(t + 1) * nbatch, :]


def _ffn_sample(l, x, g, w_up, cw, cb, w_down, buf, nbatch, t_new):
    rows, d = x.shape
    tf = TF_FFN
    ffp = w_down.shape[1]
    nj = ffp // tf
    lyr = lambda shape, imap: pl.BlockSpec((None,) + shape, lambda j: (l,) + imap(j))
    state = lambda off: pl.BlockSpec((CONV_W - 1, nbatch, tf), lambda j: (0, 0, j + off))
    return pl.pallas_call(
        functools.partial(_ffn_sample_body, nbatch=nbatch, t_new=t_new),
        grid=(nj,),
        in_specs=[pl.BlockSpec((rows, d), lambda j: (0, 0)),
                  _layer_spec(l, (1, d)),
                  lyr((d, tf), lambda j: (0, j)),
                  lyr((d, tf), lambda j: (0, j + nj)),
                  lyr((CONV_W, tf), lambda j: (0, j)),
                  lyr((CONV_W, tf), lambda j: (0, j + nj)),
                  lyr((1, tf), lambda j: (0, j)),
                  lyr((1, tf), lambda j: (0, j + nj)),
                  lyr((tf, d), lambda j: (j, 0)),
                  lyr((CONV_W - 1, nbatch, tf), lambda j: (0, 0, j)),
                  lyr((CONV_W - 1, nbatch, tf), lambda j: (0, 0, j + nj))],
        out_specs=[pl.BlockSpec((rows, d), lambda j: (0, 0)), state(0), state(0)],
        out_shape=[jax.ShapeDtypeStruct((rows, d), F32),
                   jax.ShapeDtypeStruct((CONV_W - 1, nbatch, ffp), F32),
                   jax.ShapeDtypeStruct((CONV_W - 1, nbatch, ffp), F32)],
        scratch_shapes=[pltpu.VMEM((rows, d), BF16)],
        compiler_params=_cparams(1),
        name="ffn_sample",
    )(x, g, w_up, w_up, cw, cw, cb, cb, w_down, buf, buf)


def _pad_ff_halves(a, ffp, dtype):
    d_ff = a.shape[-1] // 2
    z = jnp.zeros(a.shape[:-1] + (ffp - d_ff,), dtype)
    return jnp.concatenate([a[..., :d_ff].astype(dtype), z, a[..., d_ff:].astype(dtype), z], axis=-1)


def _unpad_ff_halves(sa, sg, d_ff):
    return jnp.concatenate([sa[..., :d_ff], sg[..., :d_ff]], axis=-1)


def kernel(x_prompt, x_sample, cache_k_win, cache_v_win, state_ffn_conv, norm_mix_g, w_in, q_norm_g, k_norm_g,
           attn_sinks, sgu_ln_g, sgu_ln_b, sgu_w, sgu_b, out_norm_att_g, out_norm_sgu_g, w_o, norm_ffn_g,
           w_up, conv_w, conv_b, w_down):
    batch, seq, d = x_prompt.shape
    nbatch, t_new, _ = x_sample.shape
    depth = w_in.shape[0]
    d_ff = w_down.shape[1]
    ffp = -(-d_ff // TF_FFN) * TF_FFN

    w_in_b = w_in.astype(BF16)
    w_o_b = w_o.astype(BF16)
    w_up_b = _pad_ff_halves(w_up, ffp, BF16)
    w_down_b = jnp.pad(w_down.astype(BF16), ((0, 0), (0, ffp - d_ff), (0, 0)))
    cw_p = _pad_ff_halves(conv_w, ffp, F32)
    cb_p = _pad_ff_halves(conv_b, ffp, F32)[:, None, :]
    row = lambda a: a[:, None, :]
    gmix, gffn = row(norm_mix_g), row(norm_ffn_g)
    lng, lnb = row(sgu_ln_g), row(sgu_ln_b)
    goa, gos = row(out_norm_att_g), row(out_norm_sgu_g)
    gq2 = row(jnp.tile(q_norm_g, (1, 2)))
    gk2 = row(jnp.tile(k_norm_g, (1, 2)))
    b_s_t = jnp.swapaxes(sgu_b, 1, 2)
    wrow = jnp.repeat(jnp.transpose(sgu_w[:, :, :t_new, :t_new], (0, 2, 3, 1)), SGU_HEAD_DIM, axis=-1)
    wrow = wrow.reshape(depth, t_new * t_new, SGU_WIDTH)
    brow = jnp.repeat(jnp.transpose(sgu_b[:, :, :t_new], (0, 2, 1)), SGU_HEAD_DIM, axis=-1)
    sink_rows = jnp.repeat(attn_sinks, t_new, axis=-1)[..., None]
    buf_tm = _pad_ff_halves(jnp.swapaxes(state_ffn_conv, 1, 2), ffp, F32)
    k_cache = cache_k_win.reshape(depth, nbatch, WINDOW, KV_WIDTH)
    v_cache = cache_v_win.reshape(depth, nbatch, WINDOW, KV_WIDTH)

    xp = x_prompt.reshape(batch * seq, d)
    xs = jnp.swapaxes(x_sample, 0, 1).reshape(t_new * nbatch, d)
    kp_l, vp_l, cp_l, ks_l, vs_l, cs_l, sv_l = [], [], [], [], [], [], []
    for l in range(depth):
        q, k, v, u, vs = _proj(l, xp, gmix, w_in_b, gq2, gk2, lng, lnb)
        mix = _mix_prompt(l, attn_sinks, q, k, v, u, vs, sgu_w, b_s_t, goa, gos, batch, seq)
        xp = _out_proj(l, xp, mix, w_o_b)
        xp, sa, sg = _ffn_prompt(l, xp, gffn, w_up_b, cw_p, cb_p, w_down_b, seq)
        kp_l.append(k.reshape(batch, seq, KV_HEADS, HEAD_DIM)[:, -WINDOW:])
        vp_l.append(v.reshape(batch, seq, KV_HEADS, HEAD_DIM)[:, -WINDOW:])
        tiles_per_seq = seq // TM_FFN
        cp_l.append(_unpad_ff_halves(sa, sg, d_ff)[tiles_per_seq - 1:batch * tiles_per_seq:tiles_per_seq])

        q, k, v, u, vs = _proj(l, xs, gmix, w_in_b, gq2, gk2, lng, lnb)
        pad_new = ((0, 0), (0, NEW_PAD - t_new), (0, 0))
        k_new = jnp.pad(jnp.swapaxes(k.reshape(t_new, nbatch, KV_WIDTH), 0, 1), pad_new)
        v_new = jnp.pad(jnp.swapaxes(v.reshape(t_new, nbatch, KV_WIDTH), 0, 1), pad_new)
        q5 = jnp.transpose(q.reshape(t_new, nbatch, ATT_HEADS, HEAD_DIM), (1, 2, 0, 3))
        q5 = q5.reshape(nbatch, ATT_HEADS * t_new, HEAD_DIM)
        q_ext = jnp.concatenate([q5, q5], axis=-1)
        o_ext, k_win, v_win = _attn_sample(l, q_ext, k_cache, k_new, v_cache, v_new, sink_rows, t_new)
        o5 = o_ext.reshape(nbatch, GROUP, t_new, KV_HEADS, HEAD_DIM)
        att = jnp.transpose(o5, (2, 0, 3, 1, 4)).reshape(t_new * nbatch, ATT_WIDTH)
        mix = _merge_sample(l, att, u, vs, wrow, brow, goa, gos, nbatch, t_new)
        xs = _out_proj(l, xs, mix, w_o_b)
        xs, sa, sg = _ffn_sample(l, xs, gffn, w_up_b, cw_p, cb_p, w_down_b, buf_tm, nbatch, t_new)
        ks_l.append(k_win.reshape(nbatch, WINDOW, KV_HEADS, HEAD_DIM))
        vs_l.append(v_win.reshape(nbatch, WINDOW, KV_HEADS, HEAD_DIM))
        cs_l.append(jnp.swapaxes(_unpad_ff_halves(sa, sg, d_ff), 0, 1))
        sv_l.append(jnp.swapaxes(vs.reshape(t_new, nbatch, SGU_HEADS, SGU_HEAD_DIM), 0, 1))

    y_prompt = xp.reshape(batch, seq, d)
    y_sample = jnp.swapaxes(xs.reshape(t_new, nbatch, d), 0, 1)
    return (y_prompt, y_sample, jnp.stack(kp_l), jnp.stack(vp_l), jnp.stack(cp_l),
            jnp.stack(ks_l), jnp.stack(vs_l), jnp.stack(cs_l), jnp.stack(sv_l))
```

```python
import functools

import jax
import jax.numpy as jnp
from jax import lax
from jax.experimental import pallas as pl
from jax.experimental.pallas import tpu as pltpu

F32 = jnp.float32
BF16 = jnp.bfloat16

HEAD_DIM = 64
ATT_HEADS = 16
KV_HEADS = 2
GROUP = ATT_HEADS // KV_HEADS
ATT_WIDTH = ATT_HEADS * HEAD_DIM
KV_WIDTH = KV_HEADS * HEAD_DIM
SGU_HEADS = 8
SGU_HEAD_DIM = 128
SGU_WIDTH = SGU_HEADS * SGU_HEAD_DIM
WINDOW = 128
BLOCK = 128
CONV_W = 3
EPS = 1e-6
NEG_INF = -1e30
LANES = 128
MXU_WIDTH = 256
HALO = 16
NEW_PAD = 16
VMEM_LIMIT = 56 * 1024 * 1024

TM_PROJ = 256
TM_OUT = 512
TM_FFN = 512
TF_FFN = 512
BB_ATTN = 16
TR_STAGE = 256

C_Q = ATT_WIDTH
C_K = C_Q + KV_WIDTH
C_V = C_K + KV_WIDTH
C_U = C_V + SGU_WIDTH


def _cparams(n_axes, flags=None):
    return pltpu.CompilerParams(dimension_semantics=("arbitrary",) * n_axes,
                                vmem_limit_bytes=VMEM_LIMIT, flags=flags)


def _layer_spec(l, shape):
    zeros = (0,) * len(shape)
    return pl.BlockSpec((None,) + tuple(shape), lambda *_: (l,) + zeros)


def _rms_rows(x, g):
    y = x * lax.rsqrt(jnp.mean(x * x, axis=-1, keepdims=True) + EPS)
    return y * g


def _rms_head_pairs(z, g2):
    lo = lax.broadcasted_iota(jnp.int32, z.shape, 1) < HEAD_DIM
    z2 = z * z
    s_lo = jnp.sum(jnp.where(lo, z2, 0.0), axis=-1, keepdims=True)
    s_hi = jnp.sum(jnp.where(lo, 0.0, z2), axis=-1, keepdims=True)
    inv = jnp.where(lo, lax.rsqrt(s_lo / HEAD_DIM + EPS), lax.rsqrt(s_hi / HEAD_DIM + EPS))
    return (z * inv) * g2


def _dot(a, b):
    return jnp.dot(a, b, preferred_element_type=F32)


def _dot_t(a, b):
    return lax.dot_general(a, b, (((1,), (1,)), ((), ())), preferred_element_type=F32)


def _proj_body(x_ref, gmix_ref, w_ref, gq_ref, gk_ref, lng_ref, lnb_ref,
               q_ref, k_ref, v_ref, u_ref, vs_ref):
    xn = _rms_rows(x_ref[...], gmix_ref[...]).astype(BF16)
    gv = jax.nn.gelu(_dot(xn, w_ref[:, C_U:]))
    mu = jnp.mean(gv, axis=-1, keepdims=True)
    xc = gv - mu
    y = xc * lax.rsqrt(jnp.mean(xc * xc, axis=-1, keepdims=True) + EPS)
    vs_ref[...] = y * lng_ref[...] + lnb_ref[...]
    u_ref[...] = jax.nn.gelu(_dot(xn, w_ref[:, C_V:C_U]))
    zq = _dot(xn, w_ref[:, 0:C_Q])
    gq = gq_ref[...]
    for c in range(ATT_WIDTH // LANES):
        sl = slice(c * LANES, (c + 1) * LANES)
        q_ref[:, sl] = (_rms_head_pairs(zq[:, sl], gq) * (HEAD_DIM ** -0.5)).astype(q_ref.dtype)
    zkv = _dot(xn, w_ref[:, C_Q:C_V])
    k_ref[...] = _rms_head_pairs(zkv[:, 0:KV_WIDTH], gk_ref[...])
    v_ref[...] = zkv[:, KV_WIDTH:]


def _proj(l, x, gmix, w_in, gq2, gk2, lng, lnb):
    rows, d = x.shape
    tm = min(TM_PROJ, rows)
    in_cols = w_in.shape[-1]
    row_spec = lambda width: pl.BlockSpec((tm, width), lambda i: (i, 0))
    return pl.pallas_call(
        _proj_body,
        grid=(rows // tm,),
        in_specs=[row_spec(d), _layer_spec(l, (1, d)), _layer_spec(l, (d, in_cols)),
                  _layer_spec(l, (1, LANES)), _layer_spec(l, (1, LANES)),
                  _layer_spec(l, (1, SGU_WIDTH)), _layer_spec(l, (1, SGU_WIDTH))],
        out_specs=[row_spec(ATT_WIDTH), row_spec(KV_WIDTH), row_spec(KV_WIDTH),
                   row_spec(SGU_WIDTH), row_spec(SGU_WIDTH)],
        out_shape=[jax.ShapeDtypeStruct((rows, ATT_WIDTH), BF16),
                   jax.ShapeDtypeStruct((rows, KV_WIDTH), F32),
                   jax.ShapeDtypeStruct((rows, KV_WIDTH), F32),
                   jax.ShapeDtypeStruct((rows, SGU_WIDTH), F32),
                   jax.ShapeDtypeStruct((rows, SGU_WIDTH), F32)],
        compiler_params=_cparams(1),
        name="proj",
    )(x, gmix, w_in, gq2, gk2, lng, lnb)


def _alibi_slope(h):
    return 2.0 ** (-8.0 * (h + 1) / ATT_HEADS)


def _softmax_with_sink(s, sink):
    m = jnp.maximum(jnp.max(s, axis=-1, keepdims=True), sink)
    e = jnp.exp(s - m)
    denom = jnp.sum(e, axis=-1, keepdims=True) + jnp.exp(sink - m)
    return e / denom


def _mix_prompt_body(sink_ref, q_ref, kc_ref, kp_ref, vc_ref, vp_ref, u_ref, vs_ref, ws_ref, bs_ref,
                     goa_ref, gos_ref, mix_ref, bias_scr, wtril_scr, *, l):
    b = pl.program_id(0)
    n = pl.program_id(1)

    @pl.when((b == 0) & (n == 0))
    def _init():
        row = lax.broadcasted_iota(jnp.int32, (BLOCK, 2 * BLOCK), 0)
        col = lax.broadcasted_iota(jnp.int32, (BLOCK, 2 * BLOCK), 1)
        dist = BLOCK + row - col
        valid = (dist >= 0) & (dist < WINDOW)
        distf = dist.astype(F32)
        for h in range(ATT_HEADS):
            bias = jnp.where(valid, -_alibi_slope(h) * distf, NEG_INF)
            bias_scr[h] = bias
            bias_scr[ATT_HEADS + h] = jnp.where(col < BLOCK, NEG_INF, bias)
        r = lax.broadcasted_iota(jnp.int32, (BLOCK, BLOCK), 0)
        c = lax.broadcasted_iota(jnp.int32, (BLOCK, BLOCK), 1)
        for h in range(SGU_HEADS):
            wtril_scr[h] = jnp.where(r >= c, ws_ref[h], 0.0).astype(BF16)

    bias_base = jnp.where(n == 0, ATT_HEADS, 0)
    lo = lax.broadcasted_iota(jnp.int32, (2 * BLOCK, LANES), 1) < HEAD_DIM
    kband = jnp.concatenate([kp_ref[...], kc_ref[...]], axis=0)
    vband = jnp.concatenate([vp_ref[...], vc_ref[...]], axis=0)
    kroll = pltpu.roll(kband, HEAD_DIM, 1)
    vroll = pltpu.roll(vband, HEAD_DIM, 1)
    q = q_ref[...]
    att_cols = []
    for g in range(KV_HEADS):
        k_src_lo, k_src_hi = (kband, kroll) if g == 0 else (kroll, kband)
        v_src_lo, v_src_hi = (vband, vroll) if g == 0 else (vroll, vband)
        k_par = (jnp.where(lo, k_src_lo, 0.0).astype(BF16), jnp.where(lo, 0.0, k_src_hi).astype(BF16))
        v_par = (jnp.where(lo, v_src_lo, 0.0).astype(BF16), jnp.where(lo, 0.0, v_src_hi).astype(BF16))
        pairs = GROUP // 2
        qp = jnp.concatenate(
            [q[:, (g * pairs + p) * LANES:(g * pairs + p + 1) * LANES] for p in range(pairs)], axis=0)
        o_pair = None
        for par in range(2):
            s_all = _dot_t(qp, k_par[par])
            probs = []
            for p in range(pairs):
                h = g * GROUP + 2 * p + par
                s = s_all[p * BLOCK:(p + 1) * BLOCK, :] + bias_scr[bias_base + h]
                probs.append(_softmax_with_sink(s, sink_ref[l, h]).astype(BF16))
            o = _dot(jnp.concatenate(probs, axis=0), v_par[par])
            o_pair = o if o_pair is None else o_pair + o
        for p in range(pairs):
            att_cols.append(o_pair[p * BLOCK:(p + 1) * BLOCK, :])
    att = jnp.concatenate(att_cols, axis=-1)
    mix_ref[:, 0:ATT_WIDTH] = _rms_rows(att, goa_ref[...]).astype(mix_ref.dtype)

    vs = vs_ref[...].astype(BF16)
    bs = bs_ref[...]
    sgu_cols = []
    for h in range(SGU_HEADS):
        sl = slice(h * SGU_HEAD_DIM, (h + 1) * SGU_HEAD_DIM)
        mixed = _dot(wtril_scr[h], vs[:, sl]) + bs[:, h:h + 1]
        sgu_cols.append(u_ref[:, sl] * mixed)
    sgu = jnp.concatenate(sgu_cols, axis=-1)
    mix_ref[:, ATT_WIDTH:] = _rms_rows(sgu, gos_ref[...]).astype(mix_ref.dtype)


def _mix_prompt(l, sinks, q, k, v, u, vs, w_s, b_s_t, goa, gos, batch, seq):
    nb = seq // BLOCK
    cur = lambda width: pl.BlockSpec((BLOCK, width), lambda b, n: (b * nb + n, 0))
    prev = lambda width: pl.BlockSpec((BLOCK, width), lambda b, n: (b * nb + jnp.maximum(n - 1, 0), 0))
    return pl.pallas_call(
        functools.partial(_mix_prompt_body, l=l),
        grid=(batch, nb),
        in_specs=[pl.BlockSpec(memory_space=pltpu.SMEM),
                  cur(ATT_WIDTH), cur(KV_WIDTH), prev(KV_WIDTH), cur(KV_WIDTH), prev(KV_WIDTH),
                  cur(SGU_WIDTH), cur(SGU_WIDTH),
                  _layer_spec(l, (SGU_HEADS, BLOCK, BLOCK)), _layer_spec(l, (BLOCK, SGU_HEADS)),
                  _layer_spec(l, (1, ATT_WIDTH)), _layer_spec(l, (1, SGU_WIDTH))],
        out_specs=cur(ATT_WIDTH + SGU_WIDTH),
        out_shape=jax.ShapeDtypeStruct((batch * seq, ATT_WIDTH + SGU_WIDTH), BF16),
        scratch_shapes=[pltpu.VMEM((2 * ATT_HEADS, BLOCK, 2 * BLOCK), F32),
                        pltpu.VMEM((SGU_HEADS, BLOCK, BLOCK), BF16)],
        compiler_params=_cparams(2),
        name="mix_prompt",
    )(sinks, q, k, k, v, v, u, vs, w_s, b_s_t, goa, gos)


def _attn_sample_body(q_ref, kc_ref, kn_ref, vc_ref, vn_ref, sink_ref, o_ref, kw_ref, vw_ref, *, t_new):
    rows = KV_HEADS * GROUP * t_new
    keys = WINDOW + NEW_PAD
    t_shift = t_new.bit_length() - 1
    assert t_new == 1 << t_shift
    r = lax.broadcasted_iota(jnp.int32, (rows, 1), 0)
    t = jnp.bitwise_and(r, t_new - 1)
    head = lax.shift_right_logical(r, t_shift)
    j = lax.broadcasted_iota(jnp.int32, (rows, keys), 1)
    dist = WINDOW + t - j
    valid = (dist >= 0) & (dist < WINDOW)
    slope = jnp.exp2(-8.0 * (head + 1).astype(F32) / ATT_HEADS)
    bias = jnp.where(valid, -slope * dist.astype(F32), NEG_INF)
    lane_lo = lax.broadcasted_iota(jnp.int32, (rows, LANES), 1) < HEAD_DIM
    row_kv0 = lax.broadcasted_iota(jnp.int32, (rows, LANES), 0) < GROUP * t_new
    q_keep = jnp.where(lane_lo == row_kv0, 1.0, 0.0).astype(BF16)

    kx = jnp.concatenate([kc_ref[...], kn_ref[...]], axis=1)
    vx = jnp.concatenate([vc_ref[...], vn_ref[...]], axis=1)
    for w_ref, c_ref, n_ref in ((kw_ref, kc_ref, kn_ref), (vw_ref, vc_ref, vn_ref)):
        w_ref[:, 0:WINDOW - t_new, :] = c_ref[:, t_new:WINDOW, :]
        w_ref[:, WINDOW - t_new:WINDOW, :] = n_ref[:, 0:t_new, :]
    q = q_ref[...] * q_keep[None]
    s = jnp.einsum('brd,bjd->brj', q, kx.astype(BF16), preferred_element_type=F32) + bias[None]
    sink = sink_ref[...][None]
    m = jnp.maximum(jnp.max(s, axis=-1, keepdims=True), sink)
    e = jnp.exp(s - m)
    denom = jnp.sum(e, axis=-1, keepdims=True) + jnp.exp(sink - m)
    p = (e / denom).astype(BF16)
    o = jnp.einsum('brj,bjd->brd', p, vx.astype(BF16), preferred_element_type=F32)
    half = GROUP * t_new
    out_lo = lax.broadcasted_iota(jnp.int32, (o.shape[0], half, LANES), 2) < HEAD_DIM
    o_ref[...] = jnp.where(out_lo, o[:, :half], o[:, half:])


def _attn_sample(l, q_ext, k_cache, k_new, v_cache, v_new, sink_rows, t_new):
    nbatch = q_ext.shape[0]
    bb = BB_ATTN
    rows = KV_HEADS * GROUP * t_new
    b3 = lambda shape: pl.BlockSpec((bb,) + shape, lambda i: (i,) + (0,) * len(shape))
    cache = pl.BlockSpec((None, bb, WINDOW, KV_WIDTH), lambda i: (l, i, 0, 0))
    return pl.pallas_call(
        functools.partial(_attn_sample_body, t_new=t_new),
        grid=(nbatch // bb,),
        in_specs=[b3((rows, LANES)), cache, b3((NEW_PAD, KV_WIDTH)), cache, b3((NEW_PAD, KV_WIDTH)),
                  _layer_spec(l, (rows, 1))],
        out_specs=[b3((rows // KV_HEADS, LANES)), b3((WINDOW, KV_WIDTH)), b3((WINDOW, KV_WIDTH))],
        out_shape=[jax.ShapeDtypeStruct((nbatch, rows // KV_HEADS, LANES), F32),
                   jax.ShapeDtypeStruct((nbatch, WINDOW, KV_WIDTH), F32),
                   jax.ShapeDtypeStruct((nbatch, WINDOW, KV_WIDTH), F32)],
        compiler_params=_cparams(1),
        name="attn_sample",
    )(q_ext, k_cache, k_new, v_cache, v_new, sink_rows)


def _merge_sample_body(att_ref, u_ref, vs_ref, wrow_ref, brow_ref, goa_ref, gos_ref, mix_ref, *, nbatch, t_new):
    mix_ref[:, 0:ATT_WIDTH] = _rms_rows(att_ref[...], goa_ref[...]).astype(mix_ref.dtype)
    for t in range(t_new):
        mixed = None
        for s in range(t + 1):
            term = wrow_ref[t * t_new + s:t * t_new + s + 1, :] * vs_ref[s * nbatch:(s + 1) * nbatch, :]
            mixed = term if mixed is None else mixed + term
        mixed = mixed + brow_ref[t:t + 1, :]
        sgu = u_ref[t * nbatch:(t + 1) * nbatch, :] * mixed
        mix_ref[t * nbatch:(t + 1) * nbatch, ATT_WIDTH:] = _rms_rows(sgu, gos_ref[...]).astype(mix_ref.dtype)


def _merge_sample(l, att, u, vs, wrow, brow, goa, gos, nbatch, t_new):
    rows = att.shape[0]
    whole = lambda a: pl.BlockSpec(a.shape, lambda i: (0,) * a.ndim)
    return pl.pallas_call(
        functools.partial(_merge_sample_body, nbatch=nbatch, t_new=t_new),
        grid=(1,),
        in_specs=[whole(att), whole(u), whole(vs), _layer_spec(l, wrow.shape[1:]), _layer_spec(l, brow.shape[1:]),
                  _layer_spec(l, (1, ATT_WIDTH)), _layer_spec(l, (1, SGU_WIDTH))],
        out_specs=pl.BlockSpec((rows, ATT_WIDTH + SGU_WIDTH), lambda i: (0, 0)),
        out_shape=jax.ShapeDtypeStruct((rows, ATT_WIDTH + SGU_WIDTH), BF16),
        compiler_params=_cparams(1),
        name="merge_sample",
    )(att, u, vs, wrow, brow, goa, gos)


def _out_proj_body(x_ref, mix_ref, w_ref, y_ref):
    y_ref[...] = x_ref[...] + _dot(mix_ref[...], w_ref[...])


def _out_proj(l, x, mix, w_o):
    rows, d = x.shape
    tm = min(TM_OUT, rows)
    return pl.pallas_call(
        _out_proj_body,
        grid=(rows // tm,),
        in_specs=[pl.BlockSpec((tm, d), lambda i: (i, 0)),
                  pl.BlockSpec((tm, mix.shape[1]), lambda i: (i, 0)),
                  _layer_spec(l, w_o.shape[1:])],
        out_specs=pl.BlockSpec((tm, d), lambda i: (i, 0)),
        out_shape=jax.ShapeDtypeStruct((rows, d), F32),
        compiler_params=_cparams(1),
        name="out_proj",
    )(x, mix, w_o)


def _silu_gate(hc_a, hc_g):
    return (jax.nn.silu(hc_g) * hc_a).astype(BF16)


def _ffn_prompt_body(x_ref, halo_ref, g_ref, wa_ref, wg_ref, cwa_ref, cwg_ref, cba_ref, cbg_ref, wd_ref,
                     y_ref, sa_ref, sg_ref, xn_scr, ha_scr, hg_scr, *, tm, tf, tiles_per_seq):
    i = pl.program_id(0)
    j = pl.program_id(1)

    @pl.when(j == 0)
    def _prologue():
        x = x_ref[...]
        xn_scr[HALO:, :] = _rms_rows(x, g_ref[...]).astype(BF16)
        keep = jnp.where((i % tiles_per_seq) == 0, 0.0, 1.0)
        xn_scr[0:HALO, :] = (_rms_rows(halo_ref[...], g_ref[...]) * keep).astype(BF16)
        y_ref[...] = x

    xe = xn_scr[...]

    def conv(h_scr, cw_ref, cb_ref, cs):
        hc = cb_ref[:, cs] + cw_ref[0:1, cs] * h_scr[pl.ds(HALO - 2, tm), cs]
        hc = hc + cw_ref[1:2, cs] * h_scr[pl.ds(HALO - 1, tm), cs]
        return hc + cw_ref[2:3, cs] * h_scr[pl.ds(HALO, tm), cs]

    for c in range(tf // MXU_WIDTH):
        cs = slice(c * MXU_WIDTH, (c + 1) * MXU_WIDTH)
        ha_scr[:, cs] = _dot(xe, wa_ref[:, cs])
        hg_scr[:, cs] = _dot(xe, wg_ref[:, cs])
    for c in range(tf // MXU_WIDTH):
        cs = slice(c * MXU_WIDTH, (c + 1) * MXU_WIDTH)
        act = _silu_gate(conv(ha_scr, cwa_ref, cba_ref, cs), conv(hg_scr, cwg_ref, cbg_ref, cs))
        y_ref[...] += _dot(act, wd_ref[cs, :])
    sa_ref[...] = ha_scr[pl.ds(HALO + tm - (CONV_W - 1), CONV_W - 1), :]
    sg_ref[...] = hg_scr[pl.ds(HALO + tm - (CONV_W - 1), CONV_W - 1), :]


def _ffn_prompt(l, x, g, w_up, cw, cb, w_down, seq):
    rows, d = x.shape
    tm, tf = TM_FFN, TF_FFN
    ffp = w_down.shape[1]
    nj = ffp // tf
    tiles_per_seq = seq // tm
    halo_blocks = tm // HALO
    lyr = lambda shape, imap: pl.BlockSpec((None,) + shape, lambda i, j: (l,) + imap(i, j))
    return pl.pallas_call(
        functools.partial(_ffn_prompt_body, tm=tm, tf=tf, tiles_per_seq=tiles_per_seq),
        grid=(rows // tm, nj),
        in_specs=[pl.BlockSpec((tm, d), lambda i, j: (i, 0)),
                  pl.BlockSpec((HALO, d), lambda i, j: (jnp.maximum(i * halo_blocks - 1, 0), 0)),
                  _layer_spec(l, (1, d)),
                  lyr((d, tf), lambda i, j: (0, j)),
                  lyr((d, tf), lambda i, j: (0, j + nj)),
                  lyr((CONV_W, tf), lambda i, j: (0, j)),
                  lyr((CONV_W, tf), lambda i, j: (0, j + nj)),
                  lyr((1, tf), lambda i, j: (0, j)),
                  lyr((1, tf), lambda i, j: (0, j + nj)),
                  lyr((tf, d), lambda i, j: (j, 0))],
        out_specs=[pl.BlockSpec((tm, d), lambda i, j: (i, 0)),
                   pl.BlockSpec((None, CONV_W - 1, tf), lambda i, j: (i, 0, j)),
                   pl.BlockSpec((None, CONV_W - 1, tf), lambda i, j: (i, 0, j))],
        out_shape=[jax.ShapeDtypeStruct((rows, d), F32),
                   jax.ShapeDtypeStruct((rows // tm, CONV_W - 1, ffp), F32),
                   jax.ShapeDtypeStruct((rows // tm, CONV_W - 1, ffp), F32)],
        scratch_shapes=[pltpu.VMEM((HALO + tm, d), BF16),
                        pltpu.VMEM((HALO + tm, tf), F32),
                        pltpu.VMEM((HALO + tm, tf), F32)],
        compiler_params=_cparams(2),
        name="ffn_prompt",
    )(x, x, g, w_up, w_up, cw, cw, cb, cb, w_down)


def _ffn_sample_body(x_ref, g_ref, wa_ref, wg_ref, cwa_ref, cwg_ref, cba_ref, cbg_ref, wd_ref, bufa_ref, bufg_ref,
                     y_ref, sa_ref, sg_ref, xn_scr, *, nbatch, t_new):
    j = pl.program_id(0)

    @pl.when(j == 0)
    def _prologue():
        x = x_ref[...]
        xn_scr[...] = _rms_rows(x, g_ref[...]).astype(BF16)
        y_ref[...] = x

    xn = xn_scr[...]
    ha = _dot(xn, wa_ref[...])
    hg = _dot(xn, wg_ref[...])

    def conv(h, buf_ref, cw_ref, cb_ref):
        hp = [buf_ref[k] for k in range(CONV_W - 1)] + [h[t * nbatch:(t + 1) * nbatch, :] for t in range(t_new)]
        out = []
        for t in range(t_new):
            hc = cb_ref[...] + cw_ref[0:1, :] * hp[t]
            for k in range(1, CONV_W):
                hc = hc + cw_ref[k:k + 1, :] * hp[t + k]
            out.append(hc)
        return jnp.concatenate(out, axis=0)

    act = _silu_gate(conv(ha, bufa_ref, cwa_ref, cba_ref), conv(hg, bufg_ref, cwg_ref, cbg_ref))
    y_ref[...] += _dot(act, wd_ref[...])
    for k in range(CONV_W - 1):
        t = t_new - (CONV_W - 1) + k
        sa_ref[k] = ha[t * nbatch:(t + 1) * nbatch, :]
        sg_ref[k] = hg[t * nbatch:(t + 1) * nbatch, :]


def _ffn_sample(l, x, g, w_up, cw, cb, w_down, buf, nbatch, t_new):
    rows, d = x.shape
    tf = TF_FFN
    ffp = w_down.shape[1]
    nj = ffp // tf
    lyr = lambda shape, imap: pl.BlockSpec((None,) + shape, lambda j: (l,) + imap(j))
    state = lambda off: pl.BlockSpec((CONV_W - 1, nbatch, tf), lambda j: (0, 0, j + off))
    return pl.pallas_call(
        functools.partial(_ffn_sample_body, nbatch=nbatch, t_new=t_new),
        grid=(nj,),
        in_specs=[pl.BlockSpec((rows, d), lambda j: (0, 0)),
                  _layer_spec(l, (1, d)),
                  lyr((d, tf), lambda j: (0, j)),
                  lyr((d, tf), lambda j: (0, j + nj)),
                  lyr((CONV_W, tf), lambda j: (0, j)),
                  lyr((CONV_W, tf), lambda j: (0, j + nj)),
                  lyr((1, tf), lambda j: (0, j)),
                  lyr((1, tf), lambda j: (0, j + nj)),
                  lyr((tf, d), lambda j: (j, 0)),
                  lyr((CONV_W - 1, nbatch, tf), lambda j: (0, 0, j)),
                  lyr((CONV_W - 1, nbatch, tf), lambda j: (0, 0, j + nj))],
        out_specs=[pl.BlockSpec((rows, d), lambda j: (0, 0)), state(0), state(0)],
        out_shape=[jax.ShapeDtypeStruct((rows, d), F32),
                   jax.ShapeDtypeStruct((CONV_W - 1, nbatch, ffp), F32),
                   jax.ShapeDtypeStruct((CONV_W - 1, nbatch, ffp), F32)],
        scratch_shapes=[pltpu.VMEM((rows, d), BF16)],
        compiler_params=_cparams(1),
        name="ffn_sample",
    )(x, g, w_up, w_up, cw, cw, cb, cb, w_down, buf, buf)


def _stage_body(w_ref, o_ref, *, axis):
    n = w_ref.shape[axis]
    if axis == 0:
        o_ref[0:n, :] = w_ref[...].astype(o_ref.dtype)
        o_ref[n:, :] = jnp.zeros((o_ref.shape[0] - n, o_ref.shape[1]), o_ref.dtype)
    else:
        o_ref[:, 0:n] = w_ref[...].astype(o_ref.dtype)
        o_ref[:, n:] = jnp.zeros((o_ref.shape[0], o_ref.shape[1] - n), o_ref.dtype)


def _stage_w_up(w_up, ffp):
    depth, d, two_ff = w_up.shape
    d_ff = two_ff // 2
    tr = TR_STAGE
    return pl.pallas_call(
        functools.partial(_stage_body, axis=1),
        grid=(depth, d // tr, 2),
        in_specs=[pl.BlockSpec((None, tr, d_ff), lambda l, r, h: (l, r, h))],
        out_specs=pl.BlockSpec((None, tr, ffp), lambda l, r, h: (l, r, h)),
        out_shape=jax.ShapeDtypeStruct((depth, d, 2 * ffp), BF16),
        compiler_params=_cparams(3),
        name="stage_w_up",
    )(w_up)


def _stage_w_down(w_down, ffp):
    depth, d_ff, d = w_down.shape
    tc = TR_STAGE
    return pl.pallas_call(
        functools.partial(_stage_body, axis=0),
        grid=(depth, d // tc),
        in_specs=[pl.BlockSpec((None, d_ff, tc), lambda l, c: (l, 0, c))],
        out_specs=pl.BlockSpec((None, ffp, tc), lambda l, c: (l, 0, c)),
        out_shape=jax.ShapeDtypeStruct((depth, ffp, d), BF16),
        compiler_params=_cparams(2),
        name="stage_w_down",
    )(w_down)


def _pad_ff_halves(a, ffp, dtype):
    d_ff = a.shape[-1] // 2
    z = jnp.zeros(a.shape[:-1] + (ffp - d_ff,), dtype)
    return jnp.concatenate([a[..., :d_ff].astype(dtype), z, a[..., d_ff:].astype(dtype), z], axis=-1)


def _unpad_ff_halves(sa, sg, d_ff):
    return jnp.concatenate([sa[..., :d_ff], sg[..., :d_ff]], axis=-1)


def kernel(x_prompt, x_sample, cache_k_win, cache_v_win, state_ffn_conv, norm_mix_g, w_in, q_norm_g, k_norm_g,
           attn_sinks, sgu_ln_g, sgu_ln_b, sgu_w, sgu_b, out_norm_att_g, out_norm_sgu_g, w_o, norm_ffn_g,
           w_up, conv_w, conv_b, w_down):
    batch, seq, d = x_prompt.shape
    nbatch, t_new, _ = x_sample.shape
    depth = w_in.shape[0]
    d_ff = w_down.shape[1]
    ffp = -(-d_ff // TF_FFN) * TF_FFN

    w_in_b = w_in.astype(BF16)
    w_o_b = w_o.astype(BF16)
    w_up_b = _stage_w_up(w_up, ffp)
    w_down_b = _stage_w_down(w_down, ffp)
    cw_p = _pad_ff_halves(conv_w, ffp, F32)
    cb_p = _pad_ff_halves(conv_b, ffp, F32)[:, None, :]
    row = lambda a: a[:, None, :]
    gmix, gffn = row(norm_mix_g), row(norm_ffn_g)
    lng, lnb = row(sgu_ln_g), row(sgu_ln_b)
    goa, gos = row(out_norm_att_g), row(out_norm_sgu_g)
    gq2 = row(jnp.tile(q_norm_g, (1, 2)))
    gk2 = row(jnp.tile(k_norm_g, (1, 2)))
    b_s_t = jnp.swapaxes(sgu_b, 1, 2)
    wrow = jnp.repeat(jnp.transpose(sgu_w[:, :, :t_new, :t_new], (0, 2, 3, 1)), SGU_HEAD_DIM, axis=-1)
    wrow = wrow.reshape(depth, t_new * t_new, SGU_WIDTH)
    brow = jnp.repeat(jnp.transpose(sgu_b[:, :, :t_new], (0, 2, 1)), SGU_HEAD_DIM, axis=-1)
    sink_rows = jnp.repeat(attn_sinks, t_new, axis=-1)[..., None]
    buf_tm = _pad_ff_halves(jnp.swapaxes(state_ffn_conv, 1, 2), ffp, F32)
    k_cache = cache_k_win.reshape(depth, nbatch, WINDOW, KV_WIDTH)
    v_cache = cache_v_win.reshape(depth, nbatch, WINDOW, KV_WIDTH)

    xp = x_prompt.reshape(batch * seq, d)
    xs = jnp.swapaxes(x_sample, 0, 1).reshape(t_new * nbatch, d)
    kp_l, vp_l, cp_l, ks_l, vs_l, cs_l, sv_l = [], [], [], [], [], [], []
    for l in range(depth):
        q, k, v, u, vs = _proj(l, xp, gmix, w_in_b, gq2, gk2, lng, lnb)
        mix = _mix_prompt(l, attn_sinks, q, k, v, u, vs, sgu_w, b_s_t, goa, gos, batch, seq)
        xp = _out_proj(l, xp, mix, w_o_b)
        xp, sa, sg = _ffn_prompt(l, xp, gffn, w_up_b, cw_p, cb_p, w_down_b, seq)
        kp_l.append(k.reshape(batch, seq, KV_HEADS, HEAD_DIM)[:, -WINDOW:])
        vp_l.append(v.reshape(batch, seq, KV_HEADS, HEAD_DIM)[:, -WINDOW:])
        tiles_per_seq = seq // TM_FFN
        cp_l.append(_unpad_ff_halves(sa, sg, d_ff)[tiles_per_seq - 1:batch * tiles_per_seq:tiles_per_seq])

        q, k, v, u, vs = _proj(l, xs, gmix, w_in_b, gq2, gk2, lng, lnb)
        pad_new = ((0, 0), (0, NEW_PAD - t_new), (0, 0))
        k_new = jnp.pad(jnp.swapaxes(k.reshape(t_new, nbatch, KV_WIDTH), 0, 1), pad_new)
        v_new = jnp.pad(jnp.swapaxes(v.reshape(t_new, nbatch, KV_WIDTH), 0, 1), pad_new)
        q5 = jnp.transpose(q.reshape(t_new, nbatch, ATT_HEADS, HEAD_DIM), (1, 2, 0, 3))
        q5 = q5.reshape(nbatch, ATT_HEADS * t_new, HEAD_DIM)
        q_ext = jnp.concatenate([q5, q5], axis=-1)
        o_ext, k_win, v_win = _attn_sample(l, q_ext, k_cache, k_new, v_cache, v_new, sink_rows, t_new)
        o5 = o_ext.reshape(nbatch, GROUP, t_new, KV_HEADS, HEAD_DIM)
        att = jnp.transpose(o5, (2, 0, 3, 1, 4)).reshape(t_new * nbatch, ATT_WIDTH)
        mix = _merge_sample(l, att, u, vs, wrow, brow, goa, gos, nbatch, t_new)
        xs = _out_proj(l, xs, mix, w_o_b)
        xs, sa, sg = _ffn_sample(l, xs, gffn, w_up_b, cw_p, cb_p, w_down_b, buf_tm, nbatch, t_new)
        ks_l.append(k_win.reshape(nbatch, WINDOW, KV_HEADS, HEAD_DIM))
        vs_l.append(v_win.reshape(nbatch, WINDOW, KV_HEADS, HEAD_DIM))
        cs_l.append(jnp.swapaxes(_unpad_ff_halves(sa, sg, d_ff), 0, 1))
        sv_l.append(jnp.swapaxes(vs.reshape(t_new, nbatch, SGU_HEADS, SGU_HEAD_DIM), 0, 1))

    y_prompt = xp.reshape(batch, seq, d)
    y_sample = jnp.swapaxes(xs.reshape(t_new, nbatch, d), 0, 1)
    return (y_prompt, y_sample, jnp.stack(kp_l), jnp.stack(vp_l), jnp.stack(cp_l),
            jnp.stack(ks_l), jnp.stack(vs_l), jnp.stack(cs_l), jnp.stack(sv_l))
```

```python
import functools

import jax
import jax.numpy as jnp
from jax import lax
from jax.experimental import pallas as pl
from jax.experimental.pallas import tpu as pltpu

F32 = jnp.float32
BF16 = jnp.bfloat16

HEAD_DIM = 64
ATT_HEADS = 16
KV_HEADS = 2
GROUP = ATT_HEADS // KV_HEADS
ATT_WIDTH = ATT_HEADS * HEAD_DIM
KV_WIDTH = KV_HEADS * HEAD_DIM
SGU_HEADS = 8
SGU_HEAD_DIM = 128
SGU_WIDTH = SGU_HEADS * SGU_HEAD_DIM
WINDOW = 128
BLOCK = 128
CONV_W = 3
EPS = 1e-6
NEG_INF = -1e30
LANES = 128
MXU_WIDTH = 256
HALO = 16
NEW_PAD = 16
VMEM_LIMIT = 56 * 1024 * 1024

TM_PROJ = 256
MIX_BLOCKS = 2
TM_OUT = 512
TM_FFN = 512
TF_FFN = 512
BB_ATTN = 16
TR_STAGE = 256

C_Q = ATT_WIDTH
C_K = C_Q + KV_WIDTH
C_V = C_K + KV_WIDTH
C_U = C_V + SGU_WIDTH


def _cparams(n_axes, flags=None):
    return pltpu.CompilerParams(dimension_semantics=("arbitrary",) * n_axes,
                                vmem_limit_bytes=VMEM_LIMIT, flags=flags)


def _layer_spec(l, shape):
    zeros = (0,) * len(shape)
    return pl.BlockSpec((None,) + tuple(shape), lambda *_: (l,) + zeros)


def _rms_rows(x, g):
    y = x * lax.rsqrt(jnp.mean(x * x, axis=-1, keepdims=True) + EPS)
    return y * g


def _rms_head_pairs(z, g2):
    lo = lax.broadcasted_iota(jnp.int32, z.shape, 1) < HEAD_DIM
    z2 = z * z
    s_lo = jnp.sum(jnp.where(lo, z2, 0.0), axis=-1, keepdims=True)
    s_hi = jnp.sum(jnp.where(lo, 0.0, z2), axis=-1, keepdims=True)
    inv = jnp.where(lo, lax.rsqrt(s_lo / HEAD_DIM + EPS), lax.rsqrt(s_hi / HEAD_DIM + EPS))
    return (z * inv) * g2


def _dot(a, b):
    return jnp.dot(a, b, preferred_element_type=F32)


def _dot_t(a, b):
    return lax.dot_general(a, b, (((1,), (1,)), ((), ())), preferred_element_type=F32)


def _proj_body(x_ref, gmix_ref, w_ref, gq_ref, gk_ref, lng_ref, lnb_ref,
               q_ref, k_ref, v_ref, u_ref, vs_ref):
    xn = _rms_rows(x_ref[...], gmix_ref[...]).astype(BF16)
    gv = jax.nn.gelu(_dot(xn, w_ref[:, C_U:]))
    mu = jnp.mean(gv, axis=-1, keepdims=True)
    xc = gv - mu
    y = xc * lax.rsqrt(jnp.mean(xc * xc, axis=-1, keepdims=True) + EPS)
    vs_ref[...] = y * lng_ref[...] + lnb_ref[...]
    u_ref[...] = jax.nn.gelu(_dot(xn, w_ref[:, C_V:C_U]))
    zq = _dot(xn, w_ref[:, 0:C_Q])
    gq = gq_ref[...]
    for c in range(ATT_WIDTH // LANES):
        sl = slice(c * LANES, (c + 1) * LANES)
        q_ref[:, sl] = (_rms_head_pairs(zq[:, sl], gq) * (HEAD_DIM ** -0.5)).astype(q_ref.dtype)
    zkv = _dot(xn, w_ref[:, C_Q:C_V])
    k_ref[...] = _rms_head_pairs(zkv[:, 0:KV_WIDTH], gk_ref[...])
    v_ref[...] = zkv[:, KV_WIDTH:]


def _proj(l, x, gmix, w_in, gq2, gk2, lng, lnb):
    rows, d = x.shape
    tm = min(TM_PROJ, rows)
    in_cols = w_in.shape[-1]
    row_spec = lambda width: pl.BlockSpec((tm, width), lambda i: (i, 0))
    return pl.pallas_call(
        _proj_body,
        grid=(rows // tm,),
        in_specs=[row_spec(d), _layer_spec(l, (1, d)), _layer_spec(l, (d, in_cols)),
                  _layer_spec(l, (1, LANES)), _layer_spec(l, (1, LANES)),
                  _layer_spec(l, (1, SGU_WIDTH)), _layer_spec(l, (1, SGU_WIDTH))],
        out_specs=[row_spec(ATT_WIDTH), row_spec(KV_WIDTH), row_spec(KV_WIDTH),
                   row_spec(SGU_WIDTH), row_spec(SGU_WIDTH)],
        out_shape=[jax.ShapeDtypeStruct((rows, ATT_WIDTH), BF16),
                   jax.ShapeDtypeStruct((rows, KV_WIDTH), F32),
                   jax.ShapeDtypeStruct((rows, KV_WIDTH), F32),
                   jax.ShapeDtypeStruct((rows, SGU_WIDTH), F32),
                   jax.ShapeDtypeStruct((rows, SGU_WIDTH), F32)],
        compiler_params=_cparams(1),
        name="proj",
    )(x, gmix, w_in, gq2, gk2, lng, lnb)


def _alibi_slope(h):
    return 2.0 ** (-8.0 * (h + 1) / ATT_HEADS)


def _softmax_with_sink(s, sink):
    m = jnp.maximum(jnp.max(s, axis=-1, keepdims=True), sink)
    e = jnp.exp(s - m)
    denom = jnp.sum(e, axis=-1, keepdims=True) + jnp.exp(sink - m)
    return e / denom


def _mix_prompt_body(sink_ref, q_ref, kc_ref, kp_ref, vc_ref, vp_ref, u_ref, vs_ref, ws_ref, bs_ref,
                     goa_ref, gos_ref, x_ref, wo_ref, y_ref, bias_scr, wtril_scr, mix0_scr, mix1_scr, *, l, nb):
    s = pl.program_id(0)

    @pl.when(s == 0)
    def _init():
        mix1_scr[...] = jnp.zeros_like(mix1_scr)
        row = lax.broadcasted_iota(jnp.int32, (BLOCK, 2 * BLOCK), 0)
        col = lax.broadcasted_iota(jnp.int32, (BLOCK, 2 * BLOCK), 1)
        dist = BLOCK + row - col
        valid = (dist >= 0) & (dist < WINDOW)
        distf = dist.astype(F32)
        for h in range(ATT_HEADS):
            bias = jnp.where(valid, -_alibi_slope(h) * distf, NEG_INF)
            bias_scr[h] = bias
            bias_scr[ATT_HEADS + h] = jnp.where(col < BLOCK, NEG_INF, bias)
        r = lax.broadcasted_iota(jnp.int32, (BLOCK, BLOCK), 0)
        c = lax.broadcasted_iota(jnp.int32, (BLOCK, BLOCK), 1)
        for h in range(SGU_HEADS):
            wtril_scr[h] = jnp.where(r >= c, ws_ref[h], 0.0).astype(BF16)

    d_out = y_ref.shape[1]
    n_sub = q_ref.shape[0] // BLOCK
    n_pieces = n_sub * KV_HEADS * 2
    piece = d_out // n_pieces
    first_of_seq = (s * n_sub) % nb == 0

    def step(mix_cur, mix_prev):
        lo = lax.broadcasted_iota(jnp.int32, (2 * BLOCK, LANES), 1) < HEAD_DIM
        mix_p = mix_prev[...]
        for t in range(n_sub):
            rows = slice(t * BLOCK, (t + 1) * BLOCK)
            if t == 0:
                bias_base = jnp.where(first_of_seq, ATT_HEADS, 0)
                k_prev, v_prev = kp_ref[...], vp_ref[...]
            else:
                bias_base = 0
                k_prev, v_prev = kc_ref[(t - 1) * BLOCK:t * BLOCK, :], vc_ref[(t - 1) * BLOCK:t * BLOCK, :]
            kband = jnp.concatenate([k_prev, kc_ref[rows, :]], axis=0)
            vband = jnp.concatenate([v_prev, vc_ref[rows, :]], axis=0)
            kroll = pltpu.roll(kband, HEAD_DIM, 1)
            vroll = pltpu.roll(vband, HEAD_DIM, 1)
            q = q_ref[rows, :]
            att_cols = []
            for g in range(KV_HEADS):
                k_src_lo, k_src_hi = (kband, kroll) if g == 0 else (kroll, kband)
                v_src_lo, v_src_hi = (vband, vroll) if g == 0 else (vroll, vband)
                k_par = (jnp.where(lo, k_src_lo, 0.0).astype(BF16), jnp.where(lo, 0.0, k_src_hi).astype(BF16))
                v_par = (jnp.where(lo, v_src_lo, 0.0).astype(BF16), jnp.where(lo, 0.0, v_src_hi).astype(BF16))
                pairs = GROUP // 2
                qp = jnp.concatenate(
                    [q[:, (g * pairs + p) * LANES:(g * pairs + p + 1) * LANES] for p in range(pairs)], axis=0)
                o_pair = None
                for par in range(2):
                    s_all = _dot_t(qp, k_par[par])
                    probs = []
                    for p in range(pairs):
                        h = g * GROUP + 2 * p + par
                        sc = s_all[p * BLOCK:(p + 1) * BLOCK, :] + bias_scr[bias_base + h]
                        probs.append(_softmax_with_sink(sc, sink_ref[l, h]).astype(BF16))
                    pc = (t * KV_HEADS + g) * 2 + par
                    cols = slice(pc * piece, (pc + 1) * piece)
                    y_ref[:, cols] = x_ref[:, cols] + _dot(mix_p, wo_ref[:, cols])
                    o = _dot(jnp.concatenate(probs, axis=0), v_par[par])
                    o_pair = o if o_pair is None else o_pair + o
                for p in range(pairs):
                    att_cols.append(o_pair[p * BLOCK:(p + 1) * BLOCK, :])
            att = jnp.concatenate(att_cols, axis=-1)
            mix_cur[rows, 0:ATT_WIDTH] = _rms_rows(att, goa_ref[...]).astype(BF16)

            vs = vs_ref[rows, :].astype(BF16)
            bs = bs_ref[...]
            sgu_cols = []
            for h in range(SGU_HEADS):
                sl = slice(h * SGU_HEAD_DIM, (h + 1) * SGU_HEAD_DIM)
                mixed = _dot(wtril_scr[h], vs[:, sl]) + bs[:, h:h + 1]
                sgu_cols.append(u_ref[rows, sl] * mixed)
            sgu = jnp.concatenate(sgu_cols, axis=-1)
            mix_cur[rows, ATT_WIDTH:] = _rms_rows(sgu, gos_ref[...]).astype(BF16)

    @pl.when(s % 2 == 0)
    def _even():
        step(mix0_scr, mix1_scr)

    @pl.when(s % 2 == 1)
    def _odd():
        step(mix1_scr, mix0_scr)


def _mix_prompt(l, sinks, q, k, v, u, vs, w_s, b_s_t, goa, gos, x, w_o, batch, seq):
    nb = seq // BLOCK
    n_sub = MIX_BLOCKS
    assert nb % n_sub == 0
    rows = n_sub * BLOCK
    n_steps = batch * nb // n_sub
    d = x.shape[1]
    cur_i = lambda s: jnp.minimum(s, n_steps - 1)
    lag = lambda s: jnp.maximum(s - 1, 0)
    cur = lambda width: pl.BlockSpec((rows, width), lambda s: (cur_i(s), 0))
    prev = lambda width: pl.BlockSpec(
        (BLOCK, width), lambda s: (cur_i(s) * n_sub - jnp.where((s * n_sub) % nb == 0, 0, 1), 0))
    lagged = pl.BlockSpec((rows, d), lambda s: (lag(s), 0))
    return pl.pallas_call(
        functools.partial(_mix_prompt_body, l=l, nb=nb),
        grid=(n_steps + 1,),
        in_specs=[pl.BlockSpec(memory_space=pltpu.SMEM),
                  cur(ATT_WIDTH), cur(KV_WIDTH), prev(KV_WIDTH), cur(KV_WIDTH), prev(KV_WIDTH),
                  cur(SGU_WIDTH), cur(SGU_WIDTH),
                  _layer_spec(l, (SGU_HEADS, BLOCK, BLOCK)), _layer_spec(l, (BLOCK, SGU_HEADS)),
                  _layer_spec(l, (1, ATT_WIDTH)), _layer_spec(l, (1, SGU_WIDTH)),
                  lagged, _layer_spec(l, w_o.shape[1:])],
        out_specs=lagged,
        out_shape=jax.ShapeDtypeStruct(x.shape, F32),
        scratch_shapes=[pltpu.VMEM((2 * ATT_HEADS, BLOCK, 2 * BLOCK), F32),
                        pltpu.VMEM((SGU_HEADS, BLOCK, BLOCK), BF16),
                        pltpu.VMEM((rows, ATT_WIDTH + SGU_WIDTH), BF16),
                        pltpu.VMEM((rows, ATT_WIDTH + SGU_WIDTH), BF16)],
        compiler_params=_cparams(1),
        name="mix_prompt",
    )(sinks, q, k, k, v, v, u, vs, w_s, b_s_t, goa, gos, x, w_o)


def _attn_sample_body(q_ref, kc_ref, kn_ref, vc_ref, vn_ref, sink_ref, o_ref, kw_ref, vw_ref, *, t_new):
    rows = KV_HEADS * GROUP * t_new
    keys = WINDOW + NEW_PAD
    t_shift = t_new.bit_length() - 1
    assert t_new == 1 << t_shift
    r = lax.broadcasted_iota(jnp.int32, (rows, 1), 0)
    t = jnp.bitwise_and(r, t_new - 1)
    head = lax.shift_right_logical(r, t_shift)
    j = lax.broadcasted_iota(jnp.int32, (rows, keys), 1)
    dist = WINDOW + t - j
    valid = (dist >= 0) & (dist < WINDOW)
    slope = jnp.exp2(-8.0 * (head + 1).astype(F32) / ATT_HEADS)
    bias = jnp.where(valid, -slope * dist.astype(F32), NEG_INF)
    lane_lo = lax.broadcasted_iota(jnp.int32, (rows, LANES), 1) < HEAD_DIM
    row_kv0 = lax.broadcasted_iota(jnp.int32, (rows, LANES), 0) < GROUP * t_new
    q_keep = jnp.where(lane_lo == row_kv0, 1.0, 0.0).astype(BF16)

    kx = jnp.concatenate([kc_ref[...], kn_ref[...]], axis=1)
    vx = jnp.concatenate([vc_ref[...], vn_ref[...]], axis=1)
    for w_ref, c_ref, n_ref in ((kw_ref, kc_ref, kn_ref), (vw_ref, vc_ref, vn_ref)):
        w_ref[:, 0:WINDOW - t_new, :] = c_ref[:, t_new:WINDOW, :]
        w_ref[:, WINDOW - t_new:WINDOW, :] = n_ref[:, 0:t_new, :]
    q = q_ref[...] * q_keep[None]
    s = jnp.einsum('brd,bjd->brj', q, kx.astype(BF16), preferred_element_type=F32) + bias[None]
    sink = sink_ref[...][None]
    m = jnp.maximum(jnp.max(s, axis=-1, keepdims=True), sink)
    e = jnp.exp(s - m)
    denom = jnp.sum(e, axis=-1, keepdims=True) + jnp.exp(sink - m)
    p = (e / denom).astype(BF16)
    o = jnp.einsum('brj,bjd->brd', p, vx.astype(BF16), preferred_element_type=F32)
    half = GROUP * t_new
    out_lo = lax.broadcasted_iota(jnp.int32, (o.shape[0], half, LANES), 2) < HEAD_DIM
    o_ref[...] = jnp.where(out_lo, o[:, :half], o[:, half:])


def _attn_sample(l, q_ext, k_cache, k_new, v_cache, v_new, sink_rows, t_new):
    nbatch = q_ext.shape[0]
    bb = BB_ATTN
    rows = KV_HEADS * GROUP * t_new
    b3 = lambda shape: pl.BlockSpec((bb,) + shape, lambda i: (i,) + (0,) * len(shape))
    cache = pl.BlockSpec((None, bb, WINDOW, KV_WIDTH), lambda i: (l, i, 0, 0))
    return pl.pallas_call(
        functools.partial(_attn_sample_body, t_new=t_new),
        grid=(nbatch // bb,),
        in_specs=[b3((rows, LANES)), cache, b3((NEW_PAD, KV_WIDTH)), cache, b3((NEW_PAD, KV_WIDTH)),
                  _layer_spec(l, (rows, 1))],
        out_specs=[b3((rows // KV_HEADS, LANES)), b3((WINDOW, KV_WIDTH)), b3((WINDOW, KV_WIDTH))],
        out_shape=[jax.ShapeDtypeStruct((nbatch, rows // KV_HEADS, LANES), F32),
                   jax.ShapeDtypeStruct((nbatch, WINDOW, KV_WIDTH), F32),
                   jax.ShapeDtypeStruct((nbatch, WINDOW, KV_WIDTH), F32)],
        compiler_params=_cparams(1),
        name="attn_sample",
    )(q_ext, k_cache, k_new, v_cache, v_new, sink_rows)


def _merge_sample_body(att_ref, u_ref, vs_ref, wrow_ref, brow_ref, goa_ref, gos_ref, mix_ref, *, nbatch, t_new):
    mix_ref[:, 0:ATT_WIDTH] = _rms_rows(att_ref[...], goa_ref[...]).astype(mix_ref.dtype)
    for t in range(t_new):
        mixed = None
        for s in range(t + 1):
            term = wrow_ref[t * t_new + s:t * t_new + s + 1, :] * vs_ref[s * nbatch:(s + 1) * nbatch, :]
            mixed = term if mixed is None else mixed + term
        mixed = mixed + brow_ref[t:t + 1, :]
        sgu = u_ref[t * nbatch:(t + 1) * nbatch, :] * mixed
        mix_ref[t * nbatch:(t + 1) * nbatch, ATT_WIDTH:] = _rms_rows(sgu, gos_ref[...]).astype(mix_ref.dtype)


def _merge_sample(l, att, u, vs, wrow, brow, goa, gos, nbatch, t_new):
    rows = att.shape[0]
    whole = lambda a: pl.BlockSpec(a.shape, lambda i: (0,) * a.ndim)
    return pl.pallas_call(
        functools.partial(_merge_sample_body, nbatch=nbatch, t_new=t_new),
        grid=(1,),
        in_specs=[whole(att), whole(u), whole(vs), _layer_spec(l, wrow.shape[1:]), _layer_spec(l, brow.shape[1:]),
                  _layer_spec(l, (1, ATT_WIDTH)), _layer_spec(l, (1, SGU_WIDTH))],
        out_specs=pl.BlockSpec((rows, ATT_WIDTH + SGU_WIDTH), lambda i: (0, 0)),
        out_shape=jax.ShapeDtypeStruct((rows, ATT_WIDTH + SGU_WIDTH), BF16),
        compiler_params=_cparams(1),
        name="merge_sample",
    )(att, u, vs, wrow, brow, goa, gos)


def _out_proj_body(x_ref, mix_ref, w_ref, y_ref):
    y_ref[...] = x_ref[...] + _dot(mix_ref[...], w_ref[...])


def _out_proj(l, x, mix, w_o):
    rows, d = x.shape
    tm = min(TM_OUT, rows)
    return pl.pallas_call(
        _out_proj_body,
        grid=(rows // tm,),
        in_specs=[pl.BlockSpec((tm, d), lambda i: (i, 0)),
                  pl.BlockSpec((tm, mix.shape[1]), lambda i: (i, 0)),
                  _layer_spec(l, w_o.shape[1:])],
        out_specs=pl.BlockSpec((tm, d), lambda i: (i, 0)),
        out_shape=jax.ShapeDtypeStruct((rows, d), F32),
        compiler_params=_cparams(1),
        name="out_proj",
    )(x, mix, w_o)


def _silu_gate(hc_a, hc_g):
    return (jax.nn.silu(hc_g) * hc_a).astype(BF16)


def _ffn_prompt_body(x_ref, halo_ref, g_ref, wa_ref, wg_ref, cwa_ref, cwg_ref, cba_ref, cbg_ref, wd_ref,
                     y_ref, sa_ref, sg_ref, xn_scr, ha_scr, hg_scr, *, tm, tf, tiles_per_seq):
    i = pl.program_id(0)
    j = pl.program_id(1)

    @pl.when(j == 0)
    def _prologue():
        x = x_ref[...]
        xn_scr[HALO:, :] = _rms_rows(x, g_ref[...]).astype(BF16)
        keep = jnp.where((i % tiles_per_seq) == 0, 0.0, 1.0)
        xn_scr[0:HALO, :] = (_rms_rows(halo_ref[...], g_ref[...]) * keep).astype(BF16)
        y_ref[...] = x

    xe = xn_scr[...]

    def conv(h_scr, cw_ref, cb_ref, cs):
        hc = cb_ref[:, cs] + cw_ref[0:1, cs] * h_scr[pl.ds(HALO - 2, tm), cs]
        hc = hc + cw_ref[1:2, cs] * h_scr[pl.ds(HALO - 1, tm), cs]
        return hc + cw_ref[2:3, cs] * h_scr[pl.ds(HALO, tm), cs]

    for c in range(tf // MXU_WIDTH):
        cs = slice(c * MXU_WIDTH, (c + 1) * MXU_WIDTH)
        ha_scr[:, cs] = _dot(xe, wa_ref[:, cs])
        hg_scr[:, cs] = _dot(xe, wg_ref[:, cs])
    for c in range(tf // MXU_WIDTH):
        cs = slice(c * MXU_WIDTH, (c + 1) * MXU_WIDTH)
        act = _silu_gate(conv(ha_scr, cwa_ref, cba_ref, cs), conv(hg_scr, cwg_ref, cbg_ref, cs))
        y_ref[...] += _dot(act, wd_ref[cs, :])
    sa_ref[...] = ha_scr[pl.ds(HALO + tm - (CONV_W - 1), CONV_W - 1), :]
    sg_ref[...] = hg_scr[pl.ds(HALO + tm - (CONV_W - 1), CONV_W - 1), :]


def _ffn_prompt(l, x, g, w_up, cw, cb, w_down, seq):
    rows, d = x.shape
    tm, tf = TM_FFN, TF_FFN
    ffp = w_down.shape[1]
    nj = ffp // tf
    tiles_per_seq = seq // tm
    halo_blocks = tm // HALO
    lyr = lambda shape, imap: pl.BlockSpec((None,) + shape, lambda i, j: (l,) + imap(i, j))
    return pl.pallas_call(
        functools.partial(_ffn_prompt_body, tm=tm, tf=tf, tiles_per_seq=tiles_per_seq),
        grid=(rows // tm, nj),
        in_specs=[pl.BlockSpec((tm, d), lambda i, j: (i, 0)),
                  pl.BlockSpec((HALO, d), lambda i, j: (jnp.maximum(i * halo_blocks - 1, 0), 0)),
                  _layer_spec(l, (1, d)),
                  lyr((d, tf), lambda i, j: (0, j)),
                  lyr((d, tf), lambda i, j: (0, j + nj)),
                  lyr((CONV_W, tf), lambda i, j: (0, j)),
                  lyr((CONV_W, tf), lambda i, j: (0, j + nj)),
                  lyr((1, tf), lambda i, j: (0, j)),
                  lyr((1, tf), lambda i, j: (0, j + nj)),
                  lyr((tf, d), lambda i, j: (j, 0))],
        out_specs=[pl.BlockSpec((tm, d), lambda i, j: (i, 0)),
                   pl.BlockSpec((None, CONV_W - 1, tf), lambda i, j: (i, 0, j)),
                   pl.BlockSpec((None, CONV_W - 1, tf), lambda i, j: (i, 0, j))],
        out_shape=[jax.ShapeDtypeStruct((rows, d), F32),
                   jax.ShapeDtypeStruct((rows // tm, CONV_W - 1, ffp), F32),
                   jax.ShapeDtypeStruct((rows // tm, CONV_W - 1, ffp), F32)],
        scratch_shapes=[pltpu.VMEM((HALO + tm, d), BF16),
                        pltpu.VMEM((HALO + tm, tf), F32),
                        pltpu.VMEM((HALO + tm, tf), F32)],
        compiler_params=_cparams(2),
        name="ffn_prompt",
    )(x, x, g, w_up, w_up, cw, cw, cb, cb, w_down)


def _ffn_sample_body(x_ref, g_ref, wa_ref, wg_ref, cwa_ref, cwg_ref, cba_ref, cbg_ref, wd_ref, bufa_ref, bufg_ref,
                     y_ref, sa_ref, sg_ref, xn_scr, *, nbatch, t_new):
    j = pl.program_id(0)

    @pl.when(j == 0)
    def _prologue():
        x = x_ref[...]
        xn_scr[...] = _rms_rows(x, g_ref[...]).astype(BF16)
        y_ref[...] = x

    xn = xn_scr[...]
    ha = _dot(xn, wa_ref[...])
    hg = _dot(xn, wg_ref[...])

    def conv(h, buf_ref, cw_ref, cb_ref):
        hp = [buf_ref[k] for k in range(CONV_W - 1)] + [h[t * nbatch:(t + 1) * nbatch, :] for t in range(t_new)]
        out = []
        for t in range(t_new):
            hc = cb_ref[...] + cw_ref[0:1, :] * hp[t]
            for k in range(1, CONV_W):
                hc = hc + cw_ref[k:k + 1, :] * hp[t + k]
            out.append(hc)
        return jnp.concatenate(out, axis=0)

    act = _silu_gate(conv(ha, bufa_ref, cwa_ref, cba_ref), conv(hg, bufg_ref, cwg_ref, cbg_ref))
    y_ref[...] += _dot(act, wd_ref[...])
    for k in range(CONV_W - 1):
        t = t_new - (CONV_W - 1) + k
        sa_ref[k] = ha[t * nbatch:(t + 1) * nbatch, :]
        sg_ref[k] = hg[t * nbatch:(t + 1) * nbatch, :]


def _ffn_sample(l, x, g, w_up, cw, cb, w_down, buf, nbatch, t_new):
    rows, d = x.shape
    tf = TF_FFN
    ffp = w_down.shape[1]
    nj = ffp // tf
    lyr = lambda shape, imap: pl.BlockSpec((None,) + shape, lambda j: (l,) + imap(j))
    state = lambda off: pl.BlockSpec((CONV_W - 1, nbatch, tf), lambda j: (0, 0, j + off))
    return pl.pallas_call(
        functools.partial(_ffn_sample_body, nbatch=nbatch, t_new=t_new),
        grid=(nj,),
        in_specs=[pl.BlockSpec((rows, d), lambda j: (0, 0)),
                  _layer_spec(l, (1, d)),
                  lyr((d, tf), lambda j: (0, j)),
                  lyr((d, tf), lambda j: (0, j + nj)),
                  lyr((CONV_W, tf), lambda j: (0, j)),
                  lyr((CONV_W, tf), lambda j: (0, j + nj)),
                  lyr((1, tf), lambda j: (0, j)),
                  lyr((1, tf), lambda j: (0, j + nj)),
                  lyr((tf, d), lambda j: (j, 0)),
                  lyr((CONV_W - 1, nbatch, tf), lambda j: (0, 0, j)),
                  lyr((CONV_W - 1, nbatch, tf), lambda j: (0, 0, j + nj))],
        out_specs=[pl.BlockSpec((rows, d), lambda j: (0, 0)), state(0), state(0)],
        out_shape=[jax.ShapeDtypeStruct((rows, d), F32),
                   jax.ShapeDtypeStruct((CONV_W - 1, nbatch, ffp), F32),
                   jax.ShapeDtypeStruct((CONV_W - 1, nbatch, ffp), F32)],
        scratch_shapes=[pltpu.VMEM((rows, d), BF16)],
        compiler_params=_cparams(1),
        name="ffn_sample",
    )(x, g, w_up, w_up, cw, cw, cb, cb, w_down, buf, buf)


def _stage_body(w_ref, o_ref, *, axis):
    n = w_ref.shape[axis]
    if axis == 0:
        o_ref[0:n, :] = w_ref[...].astype(o_ref.dtype)
        o_ref[n:, :] = jnp.zeros((o_ref.shape[0] - n, o_ref.shape[1]), o_ref.dtype)
    else:
        o_ref[:, 0:n] = w_ref[...].astype(o_ref.dtype)
        o_ref[:, n:] = jnp.zeros((o_ref.shape[0], o_ref.shape[1] - n), o_ref.dtype)


def _stage_w_up(w_up, ffp):
    depth, d, two_ff = w_up.shape
    d_ff = two_ff // 2
    tr = TR_STAGE
    return pl.pallas_call(
        functools.partial(_stage_body, axis=1),
        grid=(depth, d // tr, 2),
        in_specs=[pl.BlockSpec((None, tr, d_ff), lambda l, r, h: (l, r, h))],
        out_specs=pl.BlockSpec((None, tr, ffp), lambda l, r, h: (l, r, h)),
        out_shape=jax.ShapeDtypeStruct((depth, d, 2 * ffp), BF16),
        compiler_params=_cparams(3),
        name="stage_w_up",
    )(w_up)


def _stage_w_down(w_down, ffp):
    depth, d_ff, d = w_down.shape
    tc = TR_STAGE
    return pl.pallas_call(
        functools.partial(_stage_body, axis=0),
        grid=(depth, d // tc),
        in_specs=[pl.BlockSpec((None, d_ff, tc), lambda l, c: (l, 0, c))],
        out_specs=pl.BlockSpec((None, ffp, tc), lambda l, c: (l, 0, c)),
        out_shape=jax.ShapeDtypeStruct((depth, ffp, d), BF16),
        compiler_params=_cparams(2),
        name="stage_w_down",
    )(w_down)


def _pad_ff_halves(a, ffp, dtype):
    d_ff = a.shape[-1] // 2
    z = jnp.zeros(a.shape[:-1] + (ffp - d_ff,), dtype)
    return jnp.concatenate([a[..., :d_ff].astype(dtype), z, a[..., d_ff:].astype(dtype), z], axis=-1)


def _unpad_ff_halves(sa, sg, d_ff):
    return jnp.concatenate([sa[..., :d_ff], sg[..., :d_ff]], axis=-1)


def kernel(x_prompt, x_sample, cache_k_win, cache_v_win, state_ffn_conv, norm_mix_g, w_in, q_norm_g, k_norm_g,
           attn_sinks, sgu_ln_g, sgu_ln_b, sgu_w, sgu_b, out_norm_att_g, out_norm_sgu_g, w_o, norm_ffn_g,
           w_up, conv_w, conv_b, w_down):
    batch, seq, d = x_prompt.shape
    nbatch, t_new, _ = x_sample.shape
    depth = w_in.shape[0]
    d_ff = w_down.shape[1]
    ffp = -(-d_ff // TF_FFN) * TF_FFN

    w_in_b = w_in.astype(BF16)
    w_o_b = w_o.astype(BF16)
    w_up_b = _stage_w_up(w_up, ffp)
    w_down_b = _stage_w_down(w_down, ffp)
    cw_p = _pad_ff_halves(conv_w, ffp, F32)
    cb_p = _pad_ff_halves(conv_b, ffp, F32)[:, None, :]
    row = lambda a: a[:, None, :]
    gmix, gffn = row(norm_mix_g), row(norm_ffn_g)
    lng, lnb = row(sgu_ln_g), row(sgu_ln_b)
    goa, gos = row(out_norm_att_g), row(out_norm_sgu_g)
    gq2 = row(jnp.tile(q_norm_g, (1, 2)))
    gk2 = row(jnp.tile(k_norm_g, (1, 2)))
    b_s_t = jnp.swapaxes(sgu_b, 1, 2)
    wrow = jnp.repeat(jnp.transpose(sgu_w[:, :, :t_new, :t_new], (0, 2, 3, 1)), SGU_HEAD_DIM, axis=-1)
    wrow = wrow.reshape(depth, t_new * t_new, SGU_WIDTH)
    brow = jnp.repeat(jnp.transpose(sgu_b[:, :, :t_new], (0, 2, 1)), SGU_HEAD_DIM, axis=-1)
    sink_rows = jnp.repeat(attn_sinks, t_new, axis=-1)[..., None]
    buf_tm = _pad_ff_halves(jnp.swapaxes(state_ffn_conv, 1, 2), ffp, F32)
    k_cache = cache_k_win.reshape(depth, nbatch, WINDOW, KV_WIDTH)
    v_cache = cache_v_win.reshape(depth, nbatch, WINDOW, KV_WIDTH)

    xp = x_prompt.reshape(batch * seq, d)
    xs = jnp.swapaxes(x_sample, 0, 1).reshape(t_new * nbatch, d)
    kp_l, vp_l, cp_l, ks_l, vs_l, cs_l, sv_l = [], [], [], [], [], [], []
    for l in range(depth):
        q, k, v, u, vs = _proj(l, xp, gmix, w_in_b, gq2, gk2, lng, lnb)
        xp = _mix_prompt(l, attn_sinks, q, k, v, u, vs, sgu_w, b_s_t, goa, gos, xp, w_o_b, batch, seq)
        xp, sa, sg = _ffn_prompt(l, xp, gffn, w_up_b, cw_p, cb_p, w_down_b, seq)
        kp_l.append(k.reshape(batch, seq, KV_HEADS, HEAD_DIM)[:, -WINDOW:])
        vp_l.append(v.reshape(batch, seq, KV_HEADS, HEAD_DIM)[:, -WINDOW:])
        tiles_per_seq = seq // TM_FFN
        cp_l.append(_unpad_ff_halves(sa, sg, d_ff)[tiles_per_seq - 1:batch * tiles_per_seq:tiles_per_seq])

        q, k, v, u, vs = _proj(l, xs, gmix, w_in_b, gq2, gk2, lng, lnb)
        pad_new = ((0, 0), (0, NEW_PAD - t_new), (0, 0))
        k_new = jnp.pad(jnp.swapaxes(k.reshape(t_new, nbatch, KV_WIDTH), 0, 1), pad_new)
        v_new = jnp.pad(jnp.swapaxes(v.reshape(t_new, nbatch, KV_WIDTH), 0, 1), pad_new)
        q5 = jnp.transpose(q.reshape(t_new, nbatch, ATT_HEADS, HEAD_DIM), (1, 2, 0, 3))
        q5 = q5.reshape(nbatch, ATT_HEADS * t_new, HEAD_DIM)
        q_ext = jnp.concatenate([q5, q5], axis=-1)
        o_ext, k_win, v_win = _attn_sample(l, q_ext, k_cache, k_new, v_cache, v_new, sink_rows, t_new)
        o5 = o_ext.reshape(nbatch, GROUP, t_new, KV_HEADS, HEAD_DIM)
        att = jnp.transpose(o5, (2, 0, 3, 1, 4)).reshape(t_new * nbatch, ATT_WIDTH)
        mix = _merge_sample(l, att, u, vs, wrow, brow, goa, gos, nbatch, t_new)
        xs = _out_proj(l, xs, mix, w_o_b)
        xs, sa, sg = _ffn_sample(l, xs, gffn, w_up_b, cw_p, cb_p, w_down_b, buf_tm, nbatch, t_new)
        ks_l.append(k_win.reshape(nbatch, WINDOW, KV_HEADS, HEAD_DIM))
        vs_l.append(v_win.reshape(nbatch, WINDOW, KV_HEADS, HEAD_DIM))
        cs_l.append(jnp.swapaxes(_unpad_ff_halves(sa, sg, d_ff), 0, 1))
        sv_l.append(jnp.swapaxes(vs.reshape(t_new, nbatch, SGU_HEADS, SGU_HEAD_DIM), 0, 1))

    y_prompt = xp.reshape(batch, seq, d)
    y_sample = jnp.swapaxes(xs.reshape(t_new, nbatch, d), 0, 1)
    return (y_prompt, y_sample, jnp.stack(kp_l), jnp.stack(vp_l), jnp.stack(cp_l),
            jnp.stack(ks_l), jnp.stack(vs_l), jnp.stack(cs_l), jnp.stack(sv_l))
```

```python
import functools

import jax
import jax.numpy as jnp
from jax import lax
from jax.experimental import pallas as pl
from jax.experimental.pallas import tpu as pltpu

F32 = jnp.float32
BF16 = jnp.bfloat16

HEAD_DIM = 64
ATT_HEADS = 16
KV_HEADS = 2
GROUP = ATT_HEADS // KV_HEADS
ATT_WIDTH = ATT_HEADS * HEAD_DIM
KV_WIDTH = KV_HEADS * HEAD_DIM
SGU_HEADS = 8
SGU_HEAD_DIM = 128
SGU_WIDTH = SGU_HEADS * SGU_HEAD_DIM
WINDOW = 128
BLOCK = 128
CONV_W = 3
EPS = 1e-6
NEG_INF = -1e30
LANES = 128
MXU_WIDTH = 256
HALO = 16
NEW_PAD = 16
VMEM_LIMIT = 60 * 1024 * 1024

TM_PROJ = 256
MIX_BLOCKS = 2
TM_OUT = 512
TM_FFN = 1024
TF_FFN = 512
BB_ATTN = 16
TR_STAGE = 256

C_Q = ATT_WIDTH
C_K = C_Q + KV_WIDTH
C_V = C_K + KV_WIDTH
C_U = C_V + SGU_WIDTH


def _cparams(n_axes, flags=None):
    return pltpu.CompilerParams(dimension_semantics=("arbitrary",) * n_axes,
                                vmem_limit_bytes=VMEM_LIMIT, flags=flags)


def _layer_spec(l, shape):
    zeros = (0,) * len(shape)
    return pl.BlockSpec((None,) + tuple(shape), lambda *_: (l,) + zeros)


def _rms_rows(x, g):
    y = x * lax.rsqrt(jnp.mean(x * x, axis=-1, keepdims=True) + EPS)
    return y * g


def _rms_head_pairs(z, g2):
    lo = lax.broadcasted_iota(jnp.int32, z.shape, 1) < HEAD_DIM
    z2 = z * z
    s_lo = jnp.sum(jnp.where(lo, z2, 0.0), axis=-1, keepdims=True)
    s_hi = jnp.sum(jnp.where(lo, 0.0, z2), axis=-1, keepdims=True)
    inv = jnp.where(lo, lax.rsqrt(s_lo / HEAD_DIM + EPS), lax.rsqrt(s_hi / HEAD_DIM + EPS))
    return (z * inv) * g2


def _dot(a, b):
    return jnp.dot(a, b, preferred_element_type=F32)


def _dot_t(a, b):
    return lax.dot_general(a, b, (((1,), (1,)), ((), ())), preferred_element_type=F32)


def _proj_body(x_ref, gmix_ref, w_ref, gq_ref, gk_ref, lng_ref, lnb_ref,
               q_ref, k_ref, v_ref, u_ref, vs_ref):
    xn = _rms_rows(x_ref[...], gmix_ref[...]).astype(BF16)
    gv = jax.nn.gelu(_dot(xn, w_ref[:, C_U:]))
    mu = jnp.mean(gv, axis=-1, keepdims=True)
    xc = gv - mu
    y = xc * lax.rsqrt(jnp.mean(xc * xc, axis=-1, keepdims=True) + EPS)
    vs_ref[...] = y * lng_ref[...] + lnb_ref[...]
    u_ref[...] = jax.nn.gelu(_dot(xn, w_ref[:, C_V:C_U]))
    zq = _dot(xn, w_ref[:, 0:C_Q])
    gq = gq_ref[...]
    for c in range(ATT_WIDTH // LANES):
        sl = slice(c * LANES, (c + 1) * LANES)
        q_ref[:, sl] = (_rms_head_pairs(zq[:, sl], gq) * (HEAD_DIM ** -0.5)).astype(q_ref.dtype)
    zkv = _dot(xn, w_ref[:, C_Q:C_V])
    k_ref[...] = _rms_head_pairs(zkv[:, 0:KV_WIDTH], gk_ref[...])
    v_ref[...] = zkv[:, KV_WIDTH:]


def _proj(l, x, gmix, w_in, gq2, gk2, lng, lnb):
    rows, d = x.shape
    tm = min(TM_PROJ, rows)
    in_cols = w_in.shape[-1]
    row_spec = lambda width: pl.BlockSpec((tm, width), lambda i: (i, 0))
    return pl.pallas_call(
        _proj_body,
        grid=(rows // tm,),
        in_specs=[row_spec(d), _layer_spec(l, (1, d)), _layer_spec(l, (d, in_cols)),
                  _layer_spec(l, (1, LANES)), _layer_spec(l, (1, LANES)),
                  _layer_spec(l, (1, SGU_WIDTH)), _layer_spec(l, (1, SGU_WIDTH))],
        out_specs=[row_spec(ATT_WIDTH), row_spec(KV_WIDTH), row_spec(KV_WIDTH),
                   row_spec(SGU_WIDTH), row_spec(SGU_WIDTH)],
        out_shape=[jax.ShapeDtypeStruct((rows, ATT_WIDTH), BF16),
                   jax.ShapeDtypeStruct((rows, KV_WIDTH), F32),
                   jax.ShapeDtypeStruct((rows, KV_WIDTH), F32),
                   jax.ShapeDtypeStruct((rows, SGU_WIDTH), F32),
                   jax.ShapeDtypeStruct((rows, SGU_WIDTH), F32)],
        compiler_params=_cparams(1),
        name="proj",
    )(x, gmix, w_in, gq2, gk2, lng, lnb)


def _alibi_slope(h):
    return 2.0 ** (-8.0 * (h + 1) / ATT_HEADS)


def _softmax_with_sink(s, sink):
    m = jnp.maximum(jnp.max(s, axis=-1, keepdims=True), sink)
    e = jnp.exp(s - m)
    denom = jnp.sum(e, axis=-1, keepdims=True) + jnp.exp(sink - m)
    return e / denom


def _mix_prompt_body(sink_ref, q_ref, kc_ref, kp_ref, vc_ref, vp_ref, u_ref, vs_ref, ws_ref, bs_ref,
                     goa_ref, gos_ref, x_ref, wo_ref, y_ref, bias_scr, wtril_scr, mix0_scr, mix1_scr, *, l, nb):
    s = pl.program_id(0)

    @pl.when(s == 0)
    def _init():
        mix1_scr[...] = jnp.zeros_like(mix1_scr)
        row = lax.broadcasted_iota(jnp.int32, (BLOCK, 2 * BLOCK), 0)
        col = lax.broadcasted_iota(jnp.int32, (BLOCK, 2 * BLOCK), 1)
        dist = BLOCK + row - col
        valid = (dist >= 0) & (dist < WINDOW)
        distf = dist.astype(F32)
        for h in range(ATT_HEADS):
            bias = jnp.where(valid, -_alibi_slope(h) * distf, NEG_INF)
            bias_scr[h] = bias
            bias_scr[ATT_HEADS + h] = jnp.where(col < BLOCK, NEG_INF, bias)
        r = lax.broadcasted_iota(jnp.int32, (BLOCK, BLOCK), 0)
        c = lax.broadcasted_iota(jnp.int32, (BLOCK, BLOCK), 1)
        for h in range(SGU_HEADS):
            wtril_scr[h] = jnp.where(r >= c, ws_ref[h], 0.0).astype(BF16)

    d_out = y_ref.shape[1]
    n_sub = q_ref.shape[0] // BLOCK
    n_pieces = n_sub * KV_HEADS * 2
    piece = d_out // n_pieces
    first_of_seq = (s * n_sub) % nb == 0

    def step(mix_cur, mix_prev):
        lo = lax.broadcasted_iota(jnp.int32, (2 * BLOCK, LANES), 1) < HEAD_DIM
        mix_p = mix_prev[...]
        for t in range(n_sub):
            rows = slice(t * BLOCK, (t + 1) * BLOCK)
            if t == 0:
                bias_base = jnp.where(first_of_seq, ATT_HEADS, 0)
                k_prev, v_prev = kp_ref[...], vp_ref[...]
            else:
                bias_base = 0
                k_prev, v_prev = kc_ref[(t - 1) * BLOCK:t * BLOCK, :], vc_ref[(t - 1) * BLOCK:t * BLOCK, :]
            kband = jnp.concatenate([k_prev, kc_ref[rows, :]], axis=0)
            vband = jnp.concatenate([v_prev, vc_ref[rows, :]], axis=0)
            kroll = pltpu.roll(kband, HEAD_DIM, 1)
            vroll = pltpu.roll(vband, HEAD_DIM, 1)
            q = q_ref[rows, :]
            att_cols = []
            for g in range(KV_HEADS):
                k_src_lo, k_src_hi = (kband, kroll) if g == 0 else (kroll, kband)
                v_src_lo, v_src_hi = (vband, vroll) if g == 0 else (vroll, vband)
                k_par = (jnp.where(lo, k_src_lo, 0.0).astype(BF16), jnp.where(lo, 0.0, k_src_hi).astype(BF16))
                v_par = (jnp.where(lo, v_src_lo, 0.0).astype(BF16), jnp.where(lo, 0.0, v_src_hi).astype(BF16))
                pairs = GROUP // 2
                qp = jnp.concatenate(
                    [q[:, (g * pairs + p) * LANES:(g * pairs + p + 1) * LANES] for p in range(pairs)], axis=0)
                o_pair = None
                for par in range(2):
                    s_all = _dot_t(qp, k_par[par])
                    probs = []
                    for p in range(pairs):
                        h = g * GROUP + 2 * p + par
                        sc = s_all[p * BLOCK:(p + 1) * BLOCK, :] + bias_scr[bias_base + h]
                        probs.append(_softmax_with_sink(sc, sink_ref[l, h]).astype(BF16))
                    pc = (t * KV_HEADS + g) * 2 + par
                    cols = slice(pc * piece, (pc + 1) * piece)
                    y_ref[:, cols] = x_ref[:, cols] + _dot(mix_p, wo_ref[:, cols])
                    o = _dot(jnp.concatenate(probs, axis=0), v_par[par])
                    o_pair = o if o_pair is None else o_pair + o
                for p in range(pairs):
                    att_cols.append(o_pair[p * BLOCK:(p + 1) * BLOCK, :])
            att = jnp.concatenate(att_cols, axis=-1)
            mix_cur[rows, 0:ATT_WIDTH] = _rms_rows(att, goa_ref[...]).astype(BF16)

            vs = vs_ref[rows, :].astype(BF16)
            bs = bs_ref[...]
            sgu_cols = []
            for h in range(SGU_HEADS):
                sl = slice(h * SGU_HEAD_DIM, (h + 1) * SGU_HEAD_DIM)
                mixed = _dot(wtril_scr[h], vs[:, sl]) + bs[:, h:h + 1]
                sgu_cols.append(u_ref[rows, sl] * mixed)
            sgu = jnp.concatenate(sgu_cols, axis=-1)
            mix_cur[rows, ATT_WIDTH:] = _rms_rows(sgu, gos_ref[...]).astype(BF16)

    @pl.when(s % 2 == 0)
    def _even():
        step(mix0_scr, mix1_scr)

    @pl.when(s % 2 == 1)
    def _odd():
        step(mix1_scr, mix0_scr)


def _mix_prompt(l, sinks, q, k, v, u, vs, w_s, b_s_t, goa, gos, x, w_o, batch, seq):
    nb = seq // BLOCK
    n_sub = MIX_BLOCKS
    assert nb % n_sub == 0
    rows = n_sub * BLOCK
    n_steps = batch * nb // n_sub
    d = x.shape[1]
    cur_i = lambda s: jnp.minimum(s, n_steps - 1)
    lag = lambda s: jnp.maximum(s - 1, 0)
    cur = lambda width: pl.BlockSpec((rows, width), lambda s: (cur_i(s), 0))
    prev = lambda width: pl.BlockSpec(
        (BLOCK, width), lambda s: (cur_i(s) * n_sub - jnp.where((s * n_sub) % nb == 0, 0, 1), 0))
    lagged = pl.BlockSpec((rows, d), lambda s: (lag(s), 0))
    return pl.pallas_call(
        functools.partial(_mix_prompt_body, l=l, nb=nb),
        grid=(n_steps + 1,),
        in_specs=[pl.BlockSpec(memory_space=pltpu.SMEM),
                  cur(ATT_WIDTH), cur(KV_WIDTH), prev(KV_WIDTH), cur(KV_WIDTH), prev(KV_WIDTH),
                  cur(SGU_WIDTH), cur(SGU_WIDTH),
                  _layer_spec(l, (SGU_HEADS, BLOCK, BLOCK)), _layer_spec(l, (BLOCK, SGU_HEADS)),
                  _layer_spec(l, (1, ATT_WIDTH)), _layer_spec(l, (1, SGU_WIDTH)),
                  lagged, _layer_spec(l, w_o.shape[1:])],
        out_specs=lagged,
        out_shape=jax.ShapeDtypeStruct(x.shape, F32),
        scratch_shapes=[pltpu.VMEM((2 * ATT_HEADS, BLOCK, 2 * BLOCK), F32),
                        pltpu.VMEM((SGU_HEADS, BLOCK, BLOCK), BF16),
                        pltpu.VMEM((rows, ATT_WIDTH + SGU_WIDTH), BF16),
                        pltpu.VMEM((rows, ATT_WIDTH + SGU_WIDTH), BF16)],
        compiler_params=_cparams(1),
        name="mix_prompt",
    )(sinks, q, k, k, v, v, u, vs, w_s, b_s_t, goa, gos, x, w_o)


def _attn_sample_body(q_ref, kc_ref, kn_ref, vc_ref, vn_ref, sink_ref, o_ref, kw_ref, vw_ref, *, t_new):
    rows = KV_HEADS * GROUP * t_new
    keys = WINDOW + NEW_PAD
    t_shift = t_new.bit_length() - 1
    assert t_new == 1 << t_shift
    r = lax.broadcasted_iota(jnp.int32, (rows, 1), 0)
    t = jnp.bitwise_and(r, t_new - 1)
    head = lax.shift_right_logical(r, t_shift)
    j = lax.broadcasted_iota(jnp.int32, (rows, keys), 1)
    dist = WINDOW + t - j
    valid = (dist >= 0) & (dist < WINDOW)
    slope = jnp.exp2(-8.0 * (head + 1).astype(F32) / ATT_HEADS)
    bias = jnp.where(valid, -slope * dist.astype(F32), NEG_INF)
    lane_lo = lax.broadcasted_iota(jnp.int32, (rows, LANES), 1) < HEAD_DIM
    row_kv0 = lax.broadcasted_iota(jnp.int32, (rows, LANES), 0) < GROUP * t_new
    q_keep = jnp.where(lane_lo == row_kv0, 1.0, 0.0).astype(BF16)

    kx = jnp.concatenate([kc_ref[...], kn_ref[...]], axis=1)
    vx = jnp.concatenate([vc_ref[...], vn_ref[...]], axis=1)
    for w_ref, c_ref, n_ref in ((kw_ref, kc_ref, kn_ref), (vw_ref, vc_ref, vn_ref)):
        w_ref[:, 0:WINDOW - t_new, :] = c_ref[:, t_new:WINDOW, :]
        w_ref[:, WINDOW - t_new:WINDOW, :] = n_ref[:, 0:t_new, :]
    q = q_ref[...] * q_keep[None]
    s = jnp.einsum('brd,bjd->brj', q, kx.astype(BF16), preferred_element_type=F32) + bias[None]
    p = _softmax_with_sink(s, sink_ref[...][None]).astype(BF16)
    o = jnp.einsum('brj,bjd->brd', p, vx.astype(BF16), preferred_element_type=F32)
    half = GROUP * t_new
    out_lo = lax.broadcasted_iota(jnp.int32, (o.shape[0], half, LANES), 2) < HEAD_DIM
    o_ref[...] = jnp.where(out_lo, o[:, :half], o[:, half:])


def _attn_sample(l, q_ext, k_cache, k_new, v_cache, v_new, sink_rows, t_new):
    nbatch = q_ext.shape[0]
    bb = BB_ATTN
    rows = KV_HEADS * GROUP * t_new
    b3 = lambda shape: pl.BlockSpec((bb,) + shape, lambda i: (i,) + (0,) * len(shape))
    cache = pl.BlockSpec((None, bb, WINDOW, KV_WIDTH), lambda i: (l, i, 0, 0))
    return pl.pallas_call(
        functools.partial(_attn_sample_body, t_new=t_new),
        grid=(nbatch // bb,),
        in_specs=[b3((rows, LANES)), cache, b3((NEW_PAD, KV_WIDTH)), cache, b3((NEW_PAD, KV_WIDTH)),
                  _layer_spec(l, (rows, 1))],
        out_specs=[b3((rows // KV_HEADS, LANES)), b3((WINDOW, KV_WIDTH)), b3((WINDOW, KV_WIDTH))],
        out_shape=[jax.ShapeDtypeStruct((nbatch, rows // KV_HEADS, LANES), F32),
                   jax.ShapeDtypeStruct((nbatch, WINDOW, KV_WIDTH), F32),
                   jax.ShapeDtypeStruct((nbatch, WINDOW, KV_WIDTH), F32)],
        compiler_params=_cparams(1),
        name="attn_sample",
    )(q_ext, k_cache, k_new, v_cache, v_new, sink_rows)


def _merge_sample_body(att_ref, u_ref, vs_ref, wrow_ref, brow_ref, goa_ref, gos_ref, mix_ref, *, nbatch, t_new):
    mix_ref[:, 0:ATT_WIDTH] = _rms_rows(att_ref[...], goa_ref[...]).astype(mix_ref.dtype)
    for t in range(t_new):
        mixed = None
        for s in range(t + 1):
            term = wrow_ref[t * t_new + s:t * t_new + s + 1, :] * vs_ref[s * nbatch:(s + 1) * nbatch, :]
            mixed = term if mixed is None else mixed + term
        mixed = mixed + brow_ref[t:t + 1, :]
        sgu = u_ref[t * nbatch:(t + 1) * nbatch, :] * mixed
        mix_ref[t * nbatch:(t + 1) * nbatch, ATT_WIDTH:] = _rms_rows(sgu, gos_ref[...]).astype(mix_ref.dtype)


def _merge_sample(l, att, u, vs, wrow, brow, goa, gos, nbatch, t_new):
    rows = att.shape[0]
    whole = lambda a: pl.BlockSpec(a.shape, lambda i: (0,) * a.ndim)
    return pl.pallas_call(
        functools.partial(_merge_sample_body, nbatch=nbatch, t_new=t_new),
        grid=(1,),
        in_specs=[whole(att), whole(u), whole(vs), _layer_spec(l, wrow.shape[1:]), _layer_spec(l, brow.shape[1:]),
                  _layer_spec(l, (1, ATT_WIDTH)), _layer_spec(l, (1, SGU_WIDTH))],
        out_specs=pl.BlockSpec((rows, ATT_WIDTH + SGU_WIDTH), lambda i: (0, 0)),
        out_shape=jax.ShapeDtypeStruct((rows, ATT_WIDTH + SGU_WIDTH), BF16),
        compiler_params=_cparams(1),
        name="merge_sample",
    )(att, u, vs, wrow, brow, goa, gos)


def _out_proj_body(x_ref, mix_ref, w_ref, y_ref):
    y_ref[...] = x_ref[...] + _dot(mix_ref[...], w_ref[...])


def _out_proj(l, x, mix, w_o):
    rows, d = x.shape
    tm = min(TM_OUT, rows)
    return pl.pallas_call(
        _out_proj_body,
        grid=(rows // tm,),
        in_specs=[pl.BlockSpec((tm, d), lambda i: (i, 0)),
                  pl.BlockSpec((tm, mix.shape[1]), lambda i: (i, 0)),
                  _layer_spec(l, w_o.shape[1:])],
        out_specs=pl.BlockSpec((tm, d), lambda i: (i, 0)),
        out_shape=jax.ShapeDtypeStruct((rows, d), F32),
        compiler_params=_cparams(1),
        name="out_proj",
    )(x, mix, w_o)


def _silu_gate(hc_a, hc_g):
    return (jax.nn.silu(hc_g) * hc_a).astype(BF16)


def _ffn_prompt_body(x_ref, halo_ref, g_ref, wa_ref, wg_ref, cwa_ref, cwg_ref, cba_ref, cbg_ref, wd_ref,
                     y_ref, sa_ref, sg_ref, xn_scr, ha_scr, hg_scr, *, tm, tf, tiles_per_seq):
    i = pl.program_id(0)
    j = pl.program_id(1)

    @pl.when(j == 0)
    def _prologue():
        x = x_ref[...]
        xn_scr[HALO:, :] = _rms_rows(x, g_ref[...]).astype(BF16)
        keep = jnp.where((i % tiles_per_seq) == 0, 0.0, 1.0)
        xn_scr[0:HALO, :] = (_rms_rows(halo_ref[...], g_ref[...]) * keep).astype(BF16)
        y_ref[...] = x

    xe = xn_scr[...]

    def conv(h_scr, cw_ref, cb_ref, cs):
        hc = cb_ref[:, cs] + cw_ref[0:1, cs] * h_scr[pl.ds(HALO - 2, tm), cs]
        hc = hc + cw_ref[1:2, cs] * h_scr[pl.ds(HALO - 1, tm), cs]
        return hc + cw_ref[2:3, cs] * h_scr[pl.ds(HALO, tm), cs]

    for c in range(tf // MXU_WIDTH):
        cs = slice(c * MXU_WIDTH, (c + 1) * MXU_WIDTH)
        ha_scr[:, cs] = _dot(xe, wa_ref[:, cs])
        hg_scr[:, cs] = _dot(xe, wg_ref[:, cs])
    for c in range(tf // MXU_WIDTH):
        cs = slice(c * MXU_WIDTH, (c + 1) * MXU_WIDTH)
        act = _silu_gate(conv(ha_scr, cwa_ref, cba_ref, cs), conv(hg_scr, cwg_ref, cbg_ref, cs))
        y_ref[...] += _dot(act, wd_ref[cs, :])
    sa_ref[...] = ha_scr[pl.ds(HALO + tm - (CONV_W - 1), CONV_W - 1), :]
    sg_ref[...] = hg_scr[pl.ds(HALO + tm - (CONV_W - 1), CONV_W - 1), :]


def _ffn_prompt(l, x, g, w_up, cw, cb, w_down, seq):
    rows, d = x.shape
    tm, tf = TM_FFN, TF_FFN
    ffp = w_down.shape[1]
    nj = ffp // tf
    tiles_per_seq = seq // tm
    halo_blocks = tm // HALO
    lyr = lambda shape, imap: pl.BlockSpec((None,) + shape, lambda i, j: (l,) + imap(i, j))
    return pl.pallas_call(
        functools.partial(_ffn_prompt_body, tm=tm, tf=tf, tiles_per_seq=tiles_per_seq),
        grid=(rows // tm, nj),
        in_specs=[pl.BlockSpec((tm, d), lambda i, j: (i, 0)),
                  pl.BlockSpec((HALO, d), lambda i, j: (jnp.maximum(i * halo_blocks - 1, 0), 0)),
                  _layer_spec(l, (1, d)),
                  lyr((d, tf), lambda i, j: (0, j)),
                  lyr((d, tf), lambda i, j: (0, j + nj)),
                  lyr((CONV_W, tf), lambda i, j: (0, j)),
                  lyr((CONV_W, tf), lambda i, j: (0, j + nj)),
                  lyr((1, tf), lambda i, j: (0, j)),
                  lyr((1, tf), lambda i, j: (0, j + nj)),
                  lyr((tf, d), lambda i, j: (j, 0))],
        out_specs=[pl.BlockSpec((tm, d), lambda i, j: (i, 0)),
                   pl.BlockSpec((None, CONV_W - 1, tf), lambda i, j: (i, 0, j)),
                   pl.BlockSpec((None, CONV_W - 1, tf), lambda i, j: (i, 0, j))],
        out_shape=[jax.ShapeDtypeStruct((rows, d), F32),
                   jax.ShapeDtypeStruct((rows // tm, CONV_W - 1, ffp), F32),
                   jax.ShapeDtypeStruct((rows // tm, CONV_W - 1, ffp), F32)],
        scratch_shapes=[pltpu.VMEM((HALO + tm, d), BF16),
                        pltpu.VMEM((HALO + tm, tf), F32),
                        pltpu.VMEM((HALO + tm, tf), F32)],
        compiler_params=_cparams(2),
        name="ffn_prompt",
    )(x, x, g, w_up, w_up, cw, cw, cb, cb, w_down)


def _ffn_sample_body(x_ref, g_ref, wa_ref, wg_ref, cwa_ref, cwg_ref, cba_ref, cbg_ref, wd_ref, bufa_ref, bufg_ref,
                     y_ref, sa_ref, sg_ref, xn_scr, *, nbatch, t_new):
    j = pl.program_id(0)

    @pl.when(j == 0)
    def _prologue():
        x = x_ref[...]
        xn_scr[...] = _rms_rows(x, g_ref[...]).astype(BF16)
        y_ref[...] = x

    xn = xn_scr[...]
    ha = _dot(xn, wa_ref[...])
    hg = _dot(xn, wg_ref[...])

    def conv(h, buf_ref, cw_ref, cb_ref):
        hp = [buf_ref[k] for k in range(CONV_W - 1)] + [h[t * nbatch:(t + 1) * nbatch, :] for t in range(t_new)]
        out = []
        for t in range(t_new):
            hc = cb_ref[...] + cw_ref[0:1, :] * hp[t]
            for k in range(1, CONV_W):
                hc = hc + cw_ref[k:k + 1, :] * hp[t + k]
            out.append(hc)
        return jnp.concatenate(out, axis=0)

    act = _silu_gate(conv(ha, bufa_ref, cwa_ref, cba_ref), conv(hg, bufg_ref, cwg_ref, cbg_ref))
    y_ref[...] += _dot(act, wd_ref[...])
    for k in range(CONV_W - 1):
        t = t_new - (CONV_W - 1) + k
        sa_ref[k] = ha[t * nbatch:(t + 1) * nbatch, :]
        sg_ref[k] = hg[t * nbatch:(t + 1) * nbatch, :]


def _ffn_sample(l, x, g, w_up, cw, cb, w_down, buf, nbatch, t_new):
    rows, d = x.shape
    tf = TF_FFN
    ffp = w_down.shape[1]
    nj = ffp // tf
    lyr = lambda shape, imap: pl.BlockSpec((None,) + shape, lambda j: (l,) + imap(j))
    state = lambda off: pl.BlockSpec((CONV_W - 1, nbatch, tf), lambda j: (0, 0, j + off))
    return pl.pallas_call(
        functools.partial(_ffn_sample_body, nbatch=nbatch, t_new=t_new),
        grid=(nj,),
        in_specs=[pl.BlockSpec((rows, d), lambda j: (0, 0)),
                  _layer_spec(l, (1, d)),
                  lyr((d, tf), lambda j: (0, j)),
                  lyr((d, tf), lambda j: (0, j + nj)),
                  lyr((CONV_W, tf), lambda j: (0, j)),
                  lyr((CONV_W, tf), lambda j: (0, j + nj)),
                  lyr((1, tf), lambda j: (0, j)),
                  lyr((1, tf), lambda j: (0, j + nj)),
                  lyr((tf, d), lambda j: (j, 0)),
                  lyr((CONV_W - 1, nbatch, tf), lambda j: (0, 0, j)),
                  lyr((CONV_W - 1, nbatch, tf), lambda j: (0, 0, j + nj))],
        out_specs=[pl.BlockSpec((rows, d), lambda j: (0, 0)), state(0), state(0)],
        out_shape=[jax.ShapeDtypeStruct((rows, d), F32),
                   jax.ShapeDtypeStruct((CONV_W - 1, nbatch, ffp), F32),
                   jax.ShapeDtypeStruct((CONV_W - 1, nbatch, ffp), F32)],
        scratch_shapes=[pltpu.VMEM((rows, d), BF16)],
        compiler_params=_cparams(1),
        name="ffn_sample",
    )(x, g, w_up, w_up, cw, cw, cb, cb, w_down, buf, buf)


def _stage_body(w_ref, o_ref, *, axis):
    n = w_ref.shape[axis]
    if axis == 0:
        o_ref[0:n, :] = w_ref[...].astype(o_ref.dtype)
        o_ref[n:, :] = jnp.zeros((o_ref.shape[0] - n, o_ref.shape[1]), o_ref.dtype)
    else:
        o_ref[:, 0:n] = w_ref[...].astype(o_ref.dtype)
        o_ref[:, n:] = jnp.zeros((o_ref.shape[0], o_ref.shape[1] - n), o_ref.dtype)


def _stage_w_up(w_up, ffp):
    depth, d, two_ff = w_up.shape
    d_ff = two_ff // 2
    tr = TR_STAGE
    return pl.pallas_call(
        functools.partial(_stage_body, axis=1),
        grid=(depth, d // tr, 2),
        in_specs=[pl.BlockSpec((None, tr, d_ff), lambda l, r, h: (l, r, h))],
        out_specs=pl.BlockSpec((None, tr, ffp), lambda l, r, h: (l, r, h)),
        out_shape=jax.ShapeDtypeStruct((depth, d, 2 * ffp), BF16),
        compiler_params=_cparams(3),
        name="stage_w_up",
    )(w_up)


def _stage_w_down(w_down, ffp):
    depth, d_ff, d = w_down.shape
    tc = TR_STAGE
    return pl.pallas_call(
        functools.partial(_stage_body, axis=0),
        grid=(depth, d // tc),
        in_specs=[pl.BlockSpec((None, d_ff, tc), lambda l, c: (l, 0, c))],
        out_specs=pl.BlockSpec((None, ffp, tc), lambda l, c: (l, 0, c)),
        out_shape=jax.ShapeDtypeStruct((depth, ffp, d), BF16),
        compiler_params=_cparams(2),
        name="stage_w_down",
    )(w_down)


def _pad_ff_halves(a, ffp, dtype):
    d_ff = a.shape[-1] // 2
    z = jnp.zeros(a.shape[:-1] + (ffp - d_ff,), dtype)
    return jnp.concatenate([a[..., :d_ff].astype(dtype), z, a[..., d_ff:].astype(dtype), z], axis=-1)


def _unpad_ff_halves(sa, sg, d_ff):
    return jnp.concatenate([sa[..., :d_ff], sg[..., :d_ff]], axis=-1)


def kernel(x_prompt, x_sample, cache_k_win, cache_v_win, state_ffn_conv, norm_mix_g, w_in, q_norm_g, k_norm_g,
           attn_sinks, sgu_ln_g, sgu_ln_b, sgu_w, sgu_b, out_norm_att_g, out_norm_sgu_g, w_o, norm_ffn_g,
           w_up, conv_w, conv_b, w_down):
    batch, seq, d = x_prompt.shape
    nbatch, t_new, _ = x_sample.shape
    depth = w_in.shape[0]
    d_ff = w_down.shape[1]
    ffp = -(-d_ff // TF_FFN) * TF_FFN

    w_in_b = w_in.astype(BF16)
    w_o_b = w_o.astype(BF16)
    w_up_b = _stage_w_up(w_up, ffp)
    w_down_b = _stage_w_down(w_down, ffp)
    cw_p = _pad_ff_halves(conv_w, ffp, F32)
    cb_p = _pad_ff_halves(conv_b, ffp, F32)[:, None, :]
    row = lambda a: a[:, None, :]
    gmix, gffn = row(norm_mix_g), row(norm_ffn_g)
    lng, lnb = row(sgu_ln_g), row(sgu_ln_b)
    goa, gos = row(out_norm_att_g), row(out_norm_sgu_g)
    gq2 = row(jnp.tile(q_norm_g, (1, 2)))
    gk2 = row(jnp.tile(k_norm_g, (1, 2)))
    b_s_t = jnp.swapaxes(sgu_b, 1, 2)
    wrow = jnp.repeat(jnp.transpose(sgu_w[:, :, :t_new, :t_new], (0, 2, 3, 1)), SGU_HEAD_DIM, axis=-1)
    wrow = wrow.reshape(depth, t_new * t_new, SGU_WIDTH)
    brow = jnp.repeat(jnp.transpose(sgu_b[:, :, :t_new], (0, 2, 1)), SGU_HEAD_DIM, axis=-1)
    sink_rows = jnp.repeat(attn_sinks, t_new, axis=-1)[..., None]
    buf_tm = _pad_ff_halves(jnp.swapaxes(state_ffn_conv, 1, 2), ffp, F32)
    k_cache = cache_k_win.reshape(depth, nbatch, WINDOW, KV_WIDTH)
    v_cache = cache_v_win.reshape(depth, nbatch, WINDOW, KV_WIDTH)

    xp = x_prompt.reshape(batch * seq, d)
    xs = jnp.swapaxes(x_sample, 0, 1).reshape(t_new * nbatch, d)
    kp_l, vp_l, cp_l, ks_l, vs_l, cs_l, sv_l = [], [], [], [], [], [], []
    for l in range(depth):
        q, k, v, u, vs = _proj(l, xp, gmix, w_in_b, gq2, gk2, lng, lnb)
        xp = _mix_prompt(l, attn_sinks, q, k, v, u, vs, sgu_w, b_s_t, goa, gos, xp, w_o_b, batch, seq)
        xp, sa, sg = _ffn_prompt(l, xp, gffn, w_up_b, cw_p, cb_p, w_down_b, seq)
        kp_l.append(k.reshape(batch, seq, KV_HEADS, HEAD_DIM)[:, -WINDOW:])
        vp_l.append(v.reshape(batch, seq, KV_HEADS, HEAD_DIM)[:, -WINDOW:])
        tiles_per_seq = seq // TM_FFN
        cp_l.append(_unpad_ff_halves(sa, sg, d_ff)[tiles_per_seq - 1:batch * tiles_per_seq:tiles_per_seq])

        q, k, v, u, vs = _proj(l, xs, gmix, w_in_b, gq2, gk2, lng, lnb)
        pad_new = ((0, 0), (0, NEW_PAD - t_new), (0, 0))
        k_new = jnp.pad(jnp.swapaxes(k.reshape(t_new, nbatch, KV_WIDTH), 0, 1), pad_new)
        v_new = jnp.pad(jnp.swapaxes(v.reshape(t_new, nbatch, KV_WIDTH), 0, 1), pad_new)
        q5 = jnp.transpose(q.reshape(t_new, nbatch, ATT_HEADS, HEAD_DIM), (1, 2, 0, 3))
        q5 = q5.reshape(nbatch, ATT_HEADS * t_new, HEAD_DIM)
        q_ext = jnp.concatenate([q5, q5], axis=-1)
        o_ext, k_win, v_win = _attn_sample(l, q_ext, k_cache, k_new, v_cache, v_new, sink_rows, t_new)
        o5 = o_ext.reshape(nbatch, GROUP, t_new, KV_HEADS, HEAD_DIM)
        att = jnp.transpose(o5, (2, 0, 3, 1, 4)).reshape(t_new * nbatch, ATT_WIDTH)
        mix = _merge_sample(l, att, u, vs, wrow, brow, goa, gos, nbatch, t_new)
        xs = _out_proj(l, xs, mix, w_o_b)
        xs, sa, sg = _ffn_sample(l, xs, gffn, w_up_b, cw_p, cb_p, w_down_b, buf_tm, nbatch, t_new)
        ks_l.append(k_win.reshape(nbatch, WINDOW, KV_HEADS, HEAD_DIM))
        vs_l.append(v_win.reshape(nbatch, WINDOW, KV_HEADS, HEAD_DIM))
        cs_l.append(jnp.swapaxes(_unpad_ff_halves(sa, sg, d_ff), 0, 1))
        sv_l.append(jnp.swapaxes(vs.reshape(t_new, nbatch, SGU_HEADS, SGU_HEAD_DIM), 0, 1))

    y_prompt = xp.reshape(batch, seq, d)
    y_sample = jnp.swapaxes(xs.reshape(t_new, nbatch, d), 0, 1)
    return (y_prompt, y_sample, jnp.stack(kp_l), jnp.stack(vp_l), jnp.stack(cp_l),
            jnp.stack(ks_l), jnp.stack(vs_l), jnp.stack(cs_l), jnp.stack(sv_l))
```

```python
import functools

import jax
import jax.numpy as jnp
from jax import lax
from jax.experimental import pallas as pl
from jax.experimental.pallas import tpu as pltpu

F32 = jnp.float32
BF16 = jnp.bfloat16

HEAD_DIM = 64
ATT_HEADS = 16
KV_HEADS = 2
GROUP = ATT_HEADS // KV_HEADS
ATT_WIDTH = ATT_HEADS * HEAD_DIM
KV_WIDTH = KV_HEADS * HEAD_DIM
SGU_HEADS = 8
SGU_HEAD_DIM = 128
SGU_WIDTH = SGU_HEADS * SGU_HEAD_DIM
WINDOW = 128
BLOCK = 128
CONV_W = 3
EPS = 1e-6
NEG_INF = -1e30
LANES = 128
MXU_WIDTH = 256
HALO = 16
NEW_PAD = 16
VMEM_LIMIT = 60 * 1024 * 1024

TM_PROJ = 512
MIX_BLOCKS = 2
TM_OUT = 512
TM_FFN = 1024
TF_FFN = 512
BB_ATTN = 16
TR_STAGE = 256

C_Q = ATT_WIDTH
C_K = C_Q + KV_WIDTH
C_V = C_K + KV_WIDTH
C_U = C_V + SGU_WIDTH


def _cparams(n_axes, flags=None):
    return pltpu.CompilerParams(dimension_semantics=("arbitrary",) * n_axes,
                                vmem_limit_bytes=VMEM_LIMIT, flags=flags)


def _layer_spec(l, shape):
    zeros = (0,) * len(shape)
    return pl.BlockSpec((None,) + tuple(shape), lambda *_: (l,) + zeros)


def _rms_rows(x, g):
    y = x * lax.rsqrt(jnp.mean(x * x, axis=-1, keepdims=True) + EPS)
    return y * g


def _rms_head_pairs(z, g2):
    lo = lax.broadcasted_iota(jnp.int32, z.shape, 1) < HEAD_DIM
    z2 = z * z
    s_lo = jnp.sum(jnp.where(lo, z2, 0.0), axis=-1, keepdims=True)
    s_hi = jnp.sum(jnp.where(lo, 0.0, z2), axis=-1, keepdims=True)
    inv = jnp.where(lo, lax.rsqrt(s_lo / HEAD_DIM + EPS), lax.rsqrt(s_hi / HEAD_DIM + EPS))
    return (z * inv) * g2


def _dot(a, b):
    return jnp.dot(a, b, preferred_element_type=F32)


def _dot_t(a, b):
    return lax.dot_general(a, b, (((1,), (1,)), ((), ())), preferred_element_type=F32)


def _proj_body(x_ref, gmix_ref, w_ref, gq_ref, gk_ref, lng_ref, lnb_ref,
               q_ref, k_ref, v_ref, u_ref, vs_ref):
    xn = _rms_rows(x_ref[...], gmix_ref[...]).astype(BF16)
    gv = jax.nn.gelu(_dot(xn, w_ref[:, C_U:]))
    mu = jnp.mean(gv, axis=-1, keepdims=True)
    xc = gv - mu
    y = xc * lax.rsqrt(jnp.mean(xc * xc, axis=-1, keepdims=True) + EPS)
    vs_ref[...] = y * lng_ref[...] + lnb_ref[...]
    u_ref[...] = jax.nn.gelu(_dot(xn, w_ref[:, C_V:C_U]))
    zq = _dot(xn, w_ref[:, 0:C_Q])
    gq = gq_ref[...]
    for c in range(ATT_WIDTH // LANES):
        sl = slice(c * LANES, (c + 1) * LANES)
        q_ref[:, sl] = (_rms_head_pairs(zq[:, sl], gq) * (HEAD_DIM ** -0.5)).astype(q_ref.dtype)
    zkv = _dot(xn, w_ref[:, C_Q:C_V])
    k_ref[...] = _rms_head_pairs(zkv[:, 0:KV_WIDTH], gk_ref[...])
    v_ref[...] = zkv[:, KV_WIDTH:]


def _proj(l, x, gmix, w_in, gq2, gk2, lng, lnb):
    rows, d = x.shape
    tm = min(TM_PROJ, rows)
    in_cols = w_in.shape[-1]
    row_spec = lambda width: pl.BlockSpec((tm, width), lambda i: (i, 0))
    return pl.pallas_call(
        _proj_body,
        grid=(rows // tm,),
        in_specs=[row_spec(d), _layer_spec(l, (1, d)), _layer_spec(l, (d, in_cols)),
                  _layer_spec(l, (1, LANES)), _layer_spec(l, (1, LANES)),
                  _layer_spec(l, (1, SGU_WIDTH)), _layer_spec(l, (1, SGU_WIDTH))],
        out_specs=[row_spec(ATT_WIDTH), row_spec(KV_WIDTH), row_spec(KV_WIDTH),
                   row_spec(SGU_WIDTH), row_spec(SGU_WIDTH)],
        out_shape=[jax.ShapeDtypeStruct((rows, ATT_WIDTH), BF16),
                   jax.ShapeDtypeStruct((rows, KV_WIDTH), F32),
                   jax.ShapeDtypeStruct((rows, KV_WIDTH), F32),
                   jax.ShapeDtypeStruct((rows, SGU_WIDTH), F32),
                   jax.ShapeDtypeStruct((rows, SGU_WIDTH), F32)],
        compiler_params=_cparams(1),
        name="proj",
    )(x, gmix, w_in, gq2, gk2, lng, lnb)


def _alibi_slope(h):
    return 2.0 ** (-8.0 * (h + 1) / ATT_HEADS)


def _softmax_with_sink(s, sink):
    m = jnp.maximum(jnp.max(s, axis=-1, keepdims=True), sink)
    e = jnp.exp(s - m)
    denom = jnp.sum(e, axis=-1, keepdims=True) + jnp.exp(sink - m)
    return e / denom


def _mix_prompt_body(sink_ref, q_ref, kc_ref, kp_ref, vc_ref, vp_ref, u_ref, vs_ref, ws_ref, bs_ref,
                     goa_ref, gos_ref, x_ref, wo_ref, y_ref, bias_scr, wtril_scr, mix0_scr, mix1_scr, *, l, nb):
    s = pl.program_id(0)

    @pl.when(s == 0)
    def _init():
        mix1_scr[...] = jnp.zeros_like(mix1_scr)
        row = lax.broadcasted_iota(jnp.int32, (BLOCK, 2 * BLOCK), 0)
        col = lax.broadcasted_iota(jnp.int32, (BLOCK, 2 * BLOCK), 1)
        dist = BLOCK + row - col
        valid = (dist >= 0) & (dist < WINDOW)
        distf = dist.astype(F32)
        for h in range(ATT_HEADS):
            bias = jnp.where(valid, -_alibi_slope(h) * distf, NEG_INF)
            bias_scr[h] = bias
            bias_scr[ATT_HEADS + h] = jnp.where(col < BLOCK, NEG_INF, bias)
        r = lax.broadcasted_iota(jnp.int32, (BLOCK, BLOCK), 0)
        c = lax.broadcasted_iota(jnp.int32, (BLOCK, BLOCK), 1)
        for h in range(SGU_HEADS):
            wtril_scr[h] = jnp.where(r >= c, ws_ref[h], 0.0).astype(BF16)

    d_out = y_ref.shape[1]
    n_sub = q_ref.shape[0] // BLOCK
    n_slots = n_sub * KV_HEADS * 2
    n_pieces = min(n_slots, d_out // MXU_WIDTH)
    slots_per_piece = n_slots // n_pieces
    piece = d_out // n_pieces
    first_of_seq = (s * n_sub) % nb == 0

    def step(mix_cur, mix_prev):
        lo = lax.broadcasted_iota(jnp.int32, (2 * BLOCK, LANES), 1) < HEAD_DIM
        for t in range(n_sub):
            rows = slice(t * BLOCK, (t + 1) * BLOCK)
            if t == 0:
                bias_base = jnp.where(first_of_seq, ATT_HEADS, 0)
                k_prev, v_prev = kp_ref[...], vp_ref[...]
            else:
                bias_base = 0
                k_prev, v_prev = kc_ref[(t - 1) * BLOCK:t * BLOCK, :], vc_ref[(t - 1) * BLOCK:t * BLOCK, :]
            kband = jnp.concatenate([k_prev, kc_ref[rows, :]], axis=0)
            vband = jnp.concatenate([v_prev, vc_ref[rows, :]], axis=0)
            kroll = pltpu.roll(kband, HEAD_DIM, 1)
            vroll = pltpu.roll(vband, HEAD_DIM, 1)
            q = q_ref[rows, :]
            att_cols = []
            for g in range(KV_HEADS):
                k_src_lo, k_src_hi = (kband, kroll) if g == 0 else (kroll, kband)
                v_src_lo, v_src_hi = (vband, vroll) if g == 0 else (vroll, vband)
                k_par = (jnp.where(lo, k_src_lo, 0.0).astype(BF16), jnp.where(lo, 0.0, k_src_hi).astype(BF16))
                v_par = (jnp.where(lo, v_src_lo, 0.0).astype(BF16), jnp.where(lo, 0.0, v_src_hi).astype(BF16))
                pairs = GROUP // 2
                qp = jnp.concatenate(
                    [q[:, (g * pairs + p) * LANES:(g * pairs + p + 1) * LANES] for p in range(pairs)], axis=0)
                o_pair = None
                for par in range(2):
                    probs = []
                    for p in range(pairs):
                        h = g * GROUP + 2 * p + par
                        sc = _dot_t(qp[p * BLOCK:(p + 1) * BLOCK, :], k_par[par]) + bias_scr[bias_base + h]
                        probs.append(_softmax_with_sink(sc, sink_ref[l, h]).astype(BF16))
                    slot = (t * KV_HEADS + g) * 2 + par
                    if slot % slots_per_piece == 0:
                        pc = slot // slots_per_piece
                        cols = slice(pc * piece, (pc + 1) * piece)
                        y_ref[:, cols] = x_ref[:, cols] + _dot(mix_prev[...], wo_ref[:, cols])
                    o = _dot(jnp.concatenate(probs, axis=0), v_par[par])
                    o_pair = o if o_pair is None else o_pair + o
                for p in range(pairs):
                    att_cols.append(o_pair[p * BLOCK:(p + 1) * BLOCK, :])
            att = jnp.concatenate(att_cols, axis=-1)
            mix_cur[rows, 0:ATT_WIDTH] = _rms_rows(att, goa_ref[...]).astype(BF16)

            vs = vs_ref[rows, :].astype(BF16)
            bs = bs_ref[...]
            sgu_cols = []
            for h in range(SGU_HEADS):
                sl = slice(h * SGU_HEAD_DIM, (h + 1) * SGU_HEAD_DIM)
                mixed = _dot(wtril_scr[h], vs[:, sl]) + bs[:, h:h + 1]
                sgu_cols.append(u_ref[rows, sl] * mixed)
            sgu = jnp.concatenate(sgu_cols, axis=-1)
            mix_cur[rows, ATT_WIDTH:] = _rms_rows(sgu, gos_ref[...]).astype(BF16)

    @pl.when(s % 2 == 0)
    def _even():
        step(mix0_scr, mix1_scr)

    @pl.when(s % 2 == 1)
    def _odd():
        step(mix1_scr, mix0_scr)


def _mix_prompt(l, sinks, q, k, v, u, vs, w_s, b_s_t, goa, gos, x, w_o, batch, seq):
    nb = seq // BLOCK
    n_sub = MIX_BLOCKS
    assert nb % n_sub == 0
    rows = n_sub * BLOCK
    n_steps = batch * nb // n_sub
    d = x.shape[1]
    cur_i = lambda s: jnp.minimum(s, n_steps - 1)
    lag = lambda s: jnp.maximum(s - 1, 0)
    cur = lambda width: pl.BlockSpec((rows, width), lambda s: (cur_i(s), 0))
    prev = lambda width: pl.BlockSpec(
        (BLOCK, width), lambda s: (cur_i(s) * n_sub - jnp.where((s * n_sub) % nb == 0, 0, 1), 0))
    lagged = pl.BlockSpec((rows, d), lambda s: (lag(s), 0))
    return pl.pallas_call(
        functools.partial(_mix_prompt_body, l=l, nb=nb),
        grid=(n_steps + 1,),
        in_specs=[pl.BlockSpec(memory_space=pltpu.SMEM),
                  cur(ATT_WIDTH), cur(KV_WIDTH), prev(KV_WIDTH), cur(KV_WIDTH), prev(KV_WIDTH),
                  cur(SGU_WIDTH), cur(SGU_WIDTH),
                  _layer_spec(l, (SGU_HEADS, BLOCK, BLOCK)), _layer_spec(l, (BLOCK, SGU_HEADS)),
                  _layer_spec(l, (1, ATT_WIDTH)), _layer_spec(l, (1, SGU_WIDTH)),
                  lagged, _layer_spec(l, w_o.shape[1:])],
        out_specs=lagged,
        out_shape=jax.ShapeDtypeStruct(x.shape, F32),
        scratch_shapes=[pltpu.VMEM((2 * ATT_HEADS, BLOCK, 2 * BLOCK), F32),
                        pltpu.VMEM((SGU_HEADS, BLOCK, BLOCK), BF16),
                        pltpu.VMEM((rows, ATT_WIDTH + SGU_WIDTH), BF16),
                        pltpu.VMEM((rows, ATT_WIDTH + SGU_WIDTH), BF16)],
        compiler_params=_cparams(1),
        name="mix_prompt",
    )(sinks, q, k, k, v, v, u, vs, w_s, b_s_t, goa, gos, x, w_o)


def _attn_sample_body(q_ref, kc_ref, kn_ref, vc_ref, vn_ref, sink_ref, o_ref, kw_ref, vw_ref, *, t_new):
    rows = KV_HEADS * GROUP * t_new
    keys = WINDOW + NEW_PAD
    t_shift = t_new.bit_length() - 1
    assert t_new == 1 << t_shift
    r = lax.broadcasted_iota(jnp.int32, (rows, 1), 0)
    t = jnp.bitwise_and(r, t_new - 1)
    head = lax.shift_right_logical(r, t_shift)
    j = lax.broadcasted_iota(jnp.int32, (rows, keys), 1)
    dist = WINDOW + t - j
    valid = (dist >= 0) & (dist < WINDOW)
    slope = jnp.exp2(-8.0 * (head + 1).astype(F32) / ATT_HEADS)
    bias = jnp.where(valid, -slope * dist.astype(F32), NEG_INF)
    lane_lo = lax.broadcasted_iota(jnp.int32, (rows, LANES), 1) < HEAD_DIM
    row_kv0 = lax.broadcasted_iota(jnp.int32, (rows, LANES), 0) < GROUP * t_new
    q_keep = jnp.where(lane_lo == row_kv0, 1.0, 0.0).astype(BF16)

    kx = jnp.concatenate([kc_ref[...], kn_ref[...]], axis=1)
    vx = jnp.concatenate([vc_ref[...], vn_ref[...]], axis=1)
    for w_ref, c_ref, n_ref in ((kw_ref, kc_ref, kn_ref), (vw_ref, vc_ref, vn_ref)):
        w_ref[:, 0:WINDOW - t_new, :] = c_ref[:, t_new:WINDOW, :]
        w_ref[:, WINDOW - t_new:WINDOW, :] = n_ref[:, 0:t_new, :]
    q = q_ref[...] * q_keep[None]
    s = jnp.einsum('brd,bjd->brj', q, kx.astype(BF16), preferred_element_type=F32) + bias[None]
    p = _softmax_with_sink(s, sink_ref[...][None]).astype(BF16)
    o = jnp.einsum('brj,bjd->brd', p, vx.astype(BF16), preferred_element_type=F32)
    half = GROUP * t_new
    out_lo = lax.broadcasted_iota(jnp.int32, (o.shape[0], half, LANES), 2) < HEAD_DIM
    o_ref[...] = jnp.where(out_lo, o[:, :half], o[:, half:])


def _attn_sample(l, q_ext, k_cache, k_new, v_cache, v_new, sink_rows, t_new):
    nbatch = q_ext.shape[0]
    bb = BB_ATTN
    rows = KV_HEADS * GROUP * t_new
    b3 = lambda shape: pl.BlockSpec((bb,) + shape, lambda i: (i,) + (0,) * len(shape))
    cache = pl.BlockSpec((None, bb, WINDOW, KV_WIDTH), lambda i: (l, i, 0, 0))
    return pl.pallas_call(
        functools.partial(_attn_sample_body, t_new=t_new),
        grid=(nbatch // bb,),
        in_specs=[b3((rows, LANES)), cache, b3((NEW_PAD, KV_WIDTH)), cache, b3((NEW_PAD, KV_WIDTH)),
                  _layer_spec(l, (rows, 1))],
        out_specs=[b3((rows // KV_HEADS, LANES)), b3((WINDOW, KV_WIDTH)), b3((WINDOW, KV_WIDTH))],
        out_shape=[jax.ShapeDtypeStruct((nbatch, rows // KV_HEADS, LANES), F32),
                   jax.ShapeDtypeStruct((nbatch, WINDOW, KV_WIDTH), F32),
                   jax.ShapeDtypeStruct((nbatch, WINDOW, KV_WIDTH), F32)],
        compiler_params=_cparams(1),
        name="attn_sample",
    )(q_ext, k_cache, k_new, v_cache, v_new, sink_rows)


def _merge_sample_body(att_ref, u_ref, vs_ref, wrow_ref, brow_ref, goa_ref, gos_ref, mix_ref, *, nbatch, t_new):
    mix_ref[:, 0:ATT_WIDTH] = _rms_rows(att_ref[...], goa_ref[...]).astype(mix_ref.dtype)
    for t in range(t_new):
        mixed = None
        for s in range(t + 1):
            term = wrow_ref[t * t_new + s:t * t_new + s + 1, :] * vs_ref[s * nbatch:(s + 1) * nbatch, :]
            mixed = term if mixed is None else mixed + term
        mixed = mixed + brow_ref[t:t + 1, :]
        sgu = u_ref[t * nbatch:(t + 1) * nbatch, :] * mixed
        mix_ref[t * nbatch:(t + 1) * nbatch, ATT_WIDTH:] = _rms_rows(sgu, gos_ref[...]).astype(mix_ref.dtype)


def _merge_sample(l, att, u, vs, wrow, brow, goa, gos, nbatch, t_new):
    rows = att.shape[0]
    whole = lambda a: pl.BlockSpec(a.shape, lambda i: (0,) * a.ndim)
    return pl.pallas_call(
        functools.partial(_merge_sample_body, nbatch=nbatch, t_new=t_new),
        grid=(1,),
        in_specs=[whole(att), whole(u), whole(vs), _layer_spec(l, wrow.shape[1:]), _layer_spec(l, brow.shape[1:]),
                  _layer_spec(l, (1, ATT_WIDTH)), _layer_spec(l, (1, SGU_WIDTH))],
        out_specs=pl.BlockSpec((rows, ATT_WIDTH + SGU_WIDTH), lambda i: (0, 0)),
        out_shape=jax.ShapeDtypeStruct((rows, ATT_WIDTH + SGU_WIDTH), BF16),
        compiler_params=_cparams(1),
        name="merge_sample",
    )(att, u, vs, wrow, brow, goa, gos)


def _out_proj_body(x_ref, mix_ref, w_ref, y_ref):
    y_ref[...] = x_ref[...] + _dot(mix_ref[...], w_ref[...])


def _out_proj(l, x, mix, w_o):
    rows, d = x.shape
    tm = min(TM_OUT, rows)
    return pl.pallas_call(
        _out_proj_body,
        grid=(rows // tm,),
        in_specs=[pl.BlockSpec((tm, d), lambda i: (i, 0)),
                  pl.BlockSpec((tm, mix.shape[1]), lambda i: (i, 0)),
                  _layer_spec(l, w_o.shape[1:])],
        out_specs=pl.BlockSpec((tm, d), lambda i: (i, 0)),
        out_shape=jax.ShapeDtypeStruct((rows, d), F32),
        compiler_params=_cparams(1),
        name="out_proj",
    )(x, mix, w_o)


def _silu_gate(hc_a, hc_g):
    return (jax.nn.silu(hc_g) * hc_a).astype(BF16)


def _ffn_prompt_body(x_ref, halo_ref, g_ref, wa_ref, wg_ref, cwa_ref, cwg_ref, cba_ref, cbg_ref, wd_ref,
                     y_ref, sa_ref, sg_ref, xn_scr, ha_scr, hg_scr, *, tm, tf, tiles_per_seq):
    i = pl.program_id(0)
    j = pl.program_id(1)

    @pl.when(j == 0)
    def _prologue():
        x = x_ref[...]
        xn_scr[HALO:, :] = _rms_rows(x, g_ref[...]).astype(BF16)
        keep = jnp.where((i % tiles_per_seq) == 0, 0.0, 1.0)
        xn_scr[0:HALO, :] = (_rms_rows(halo_ref[...], g_ref[...]) * keep).astype(BF16)
        y_ref[...] = x

    xe = xn_scr[...]

    def conv(h_scr, cw_ref, cb_ref, cs, r0, nr):
        hc = cb_ref[:, cs] + cw_ref[0:1, cs] * h_scr[pl.ds(HALO - 2 + r0, nr), cs]
        hc = hc + cw_ref[1:2, cs] * h_scr[pl.ds(HALO - 1 + r0, nr), cs]
        return hc + cw_ref[2:3, cs] * h_scr[pl.ds(HALO + r0, nr), cs]

    n_chunks = tf // MXU_WIDTH
    for c in range(n_chunks):
        cs = slice(c * MXU_WIDTH, (c + 1) * MXU_WIDTH)
        ha_scr[:, cs] = _dot(xe, wa_ref[:, cs])
        hg_scr[:, cs] = _dot(xe, wg_ref[:, cs])
    for c in range(n_chunks):
        cs = slice(c * MXU_WIDTH, (c + 1) * MXU_WIDTH)
        n_row = 2 if c == n_chunks - 1 else 1
        nr = tm // n_row
        for r in range(n_row):
            act = _silu_gate(conv(ha_scr, cwa_ref, cba_ref, cs, r * nr, nr),
                             conv(hg_scr, cwg_ref, cbg_ref, cs, r * nr, nr))
            y_ref[r * nr:(r + 1) * nr, :] += _dot(act, wd_ref[cs, :])
    sa_ref[...] = ha_scr[pl.ds(HALO + tm - (CONV_W - 1), CONV_W - 1), :]
    sg_ref[...] = hg_scr[pl.ds(HALO + tm - (CONV_W - 1), CONV_W - 1), :]


def _ffn_prompt(l, x, g, w_up, cw, cb, w_down, seq):
    rows, d = x.shape
    tm, tf = TM_FFN, TF_FFN
    ffp = w_down.shape[1]
    nj = ffp // tf
    tiles_per_seq = seq // tm
    halo_blocks = tm // HALO
    lyr = lambda shape, imap: pl.BlockSpec((None,) + shape, lambda i, j: (l,) + imap(i, j))
    return pl.pallas_call(
        functools.partial(_ffn_prompt_body, tm=tm, tf=tf, tiles_per_seq=tiles_per_seq),
        grid=(rows // tm, nj),
        in_specs=[pl.BlockSpec((tm, d), lambda i, j: (i, 0)),
                  pl.BlockSpec((HALO, d), lambda i, j: (jnp.maximum(i * halo_blocks - 1, 0), 0)),
                  _layer_spec(l, (1, d)),
                  lyr((d, tf), lambda i, j: (0, j)),
                  lyr((d, tf), lambda i, j: (0, j + nj)),
                  lyr((CONV_W, tf), lambda i, j: (0, j)),
                  lyr((CONV_W, tf), lambda i, j: (0, j + nj)),
                  lyr((1, tf), lambda i, j: (0, j)),
                  lyr((1, tf), lambda i, j: (0, j + nj)),
                  lyr((tf, d), lambda i, j: (j, 0))],
        out_specs=[pl.BlockSpec((tm, d), lambda i, j: (i, 0)),
                   pl.BlockSpec((None, CONV_W - 1, tf), lambda i, j: (i, 0, j)),
                   pl.BlockSpec((None, CONV_W - 1, tf), lambda i, j: (i, 0, j))],
        out_shape=[jax.ShapeDtypeStruct((rows, d), F32),
                   jax.ShapeDtypeStruct((rows // tm, CONV_W - 1, ffp), F32),
                   jax.ShapeDtypeStruct((rows // tm, CONV_W - 1, ffp), F32)],
        scratch_shapes=[pltpu.VMEM((HALO + tm, d), BF16),
                        pltpu.VMEM((HALO + tm, tf), F32),
                        pltpu.VMEM((HALO + tm, tf), F32)],
        compiler_params=_cparams(2),
        name="ffn_prompt",
    )(x, x, g, w_up, w_up, cw, cw, cb, cb, w_down)


def _ffn_sample_body(x_ref, g_ref, wa_ref, wg_ref, cwa_ref, cwg_ref, cba_ref, cbg_ref, wd_ref, bufa_ref, bufg_ref,
                     y_ref, sa_ref, sg_ref, xn_scr, *, nbatch, t_new):
    j = pl.program_id(0)

    @pl.when(j == 0)
    def _prologue():
        x = x_ref[...]
        xn_scr[...] = _rms_rows(x, g_ref[...]).astype(BF16)
        y_ref[...] = x

    xn = xn_scr[...]
    ha = _dot(xn, wa_ref[...])
    hg = _dot(xn, wg_ref[...])

    def conv(h, buf_ref, cw_ref, cb_ref):
        hp = [buf_ref[k] for k in range(CONV_W - 1)] + [h[t * nbatch:(t + 1) * nbatch, :] for t in range(t_new)]
        out = []
        for t in range(t_new):
            hc = cb_ref[...] + cw_ref[0:1, :] * hp[t]
            for k in range(1, CONV_W):
                hc = hc + cw_ref[k:k + 1, :] * hp[t + k]
            out.append(hc)
        return jnp.concatenate(out, axis=0)

    act = _silu_gate(conv(ha, bufa_ref, cwa_ref, cba_ref), conv(hg, bufg_ref, cwg_ref, cbg_ref))
    y_ref[...] += _dot(act, wd_ref[...])
    for k in range(CONV_W - 1):
        t = t_new - (CONV_W - 1) + k
        sa_ref[k] = ha[t * nbatch:(t + 1) * nbatch, :]
        sg_ref[k] = hg[t * nbatch:(t + 1) * nbatch, :]


def _ffn_sample(l, x, g, w_up, cw, cb, w_down, buf, nbatch, t_new):
    rows, d = x.shape
    tf = TF_FFN
    ffp = w_down.shape[1]
    nj = ffp // tf
    lyr = lambda shape, imap: pl.BlockSpec((None,) + shape, lambda j: (l,) + imap(j))
    state = lambda off: pl.BlockSpec((CONV_W - 1, nbatch, tf), lambda j: (0, 0, j + off))
    return pl.pallas_call(
        functools.partial(_ffn_sample_body, nbatch=nbatch, t_new=t_new),
        grid=(nj,),
        in_specs=[pl.BlockSpec((rows, d), lambda j: (0, 0)),
                  _layer_spec(l, (1, d)),
                  lyr((d, tf), lambda j: (0, j)),
                  lyr((d, tf), lambda j: (0, j + nj)),
                  lyr((CONV_W, tf), lambda j: (0, j)),
                  lyr((CONV_W, tf), lambda j: (0, j + nj)),
                  lyr((1, tf), lambda j: (0, j)),
                  lyr((1, tf), lambda j: (0, j + nj)),
                  lyr((tf, d), lambda j: (j, 0)),
                  lyr((CONV_W - 1, nbatch, tf), lambda j: (0, 0, j)),
                  lyr((CONV_W - 1, nbatch, tf), lambda j: (0, 0, j + nj))],
        out_specs=[pl.BlockSpec((rows, d), lambda j: (0, 0)), state(0), state(0)],
        out_shape=[jax.ShapeDtypeStruct((rows, d), F32),
                   jax.ShapeDtypeStruct((CONV_W - 1, nbatch, ffp), F32),
                   jax.ShapeDtypeStruct((CONV_W - 1, nbatch, ffp), F32)],
        scratch_shapes=[pltpu.VMEM((rows, d), BF16)],
        compiler_params=_cparams(1),
        name="ffn_sample",
    )(x, g, w_up, w_up, cw, cw, cb, cb, w_down, buf, buf)


def _stage_body(w_ref, o_ref, *, axis):
    n = w_ref.shape[axis]
    if axis == 0:
        o_ref[0:n, :] = w_ref[...].astype(o_ref.dtype)
        o_ref[n:, :] = jnp.zeros((o_ref.shape[0] - n, o_ref.shape[1]), o_ref.dtype)
    else:
        o_ref[:, 0:n] = w_ref[...].astype(o_ref.dtype)
        o_ref[:, n:] = jnp.zeros((o_ref.shape[0], o_ref.shape[1] - n), o_ref.dtype)


def _stage_w_up(w_up, ffp):
    depth, d, two_ff = w_up.shape
    d_ff = two_ff // 2
    tr = TR_STAGE
    return pl.pallas_call(
        functools.partial(_stage_body, axis=1),
        grid=(depth, d // tr, 2),
        in_specs=[pl.BlockSpec((None, tr, d_ff), lambda l, r, h: (l, r, h))],
        out_specs=pl.BlockSpec((None, tr, ffp), lambda l, r, h: (l, r, h)),
        out_shape=jax.ShapeDtypeStruct((depth, d, 2 * ffp), BF16),
        compiler_params=_cparams(3),
        name="stage_w_up",
    )(w_up)


def _stage_w_down(w_down, ffp):
    depth, d_ff, d = w_down.shape
    tc = TR_STAGE
    return pl.pallas_call(
        functools.partial(_stage_body, axis=0),
        grid=(depth, d // tc),
        in_specs=[pl.BlockSpec((None, d_ff, tc), lambda l, c: (l, 0, c))],
        out_specs=pl.BlockSpec((None, ffp, tc), lambda l, c: (l, 0, c)),
        out_shape=jax.ShapeDtypeStruct((depth, ffp, d), BF16),
        compiler_params=_cparams(2),
        name="stage_w_down",
    )(w_down)


def _pad_ff_halves(a, ffp, dtype):
    d_ff = a.shape[-1] // 2
    z = jnp.zeros(a.shape[:-1] + (ffp - d_ff,), dtype)
    return jnp.concatenate([a[..., :d_ff].astype(dtype), z, a[..., d_ff:].astype(dtype), z], axis=-1)


def _unpad_ff_halves(sa, sg, d_ff):
    return jnp.concatenate([sa[..., :d_ff], sg[..., :d_ff]], axis=-1)


def kernel(x_prompt, x_sample, cache_k_win, cache_v_win, state_ffn_conv, norm_mix_g, w_in, q_norm_g, k_norm_g,
           attn_sinks, sgu_ln_g, sgu_ln_b, sgu_w, sgu_b, out_norm_att_g, out_norm_sgu_g, w_o, norm_ffn_g,
           w_up, conv_w, conv_b, w_down):
    batch, seq, d = x_prompt.shape
    nbatch, t_new, _ = x_sample.shape
    depth = w_in.shape[0]
    d_ff = w_down.shape[1]
    ffp = -(-d_ff // TF_FFN) * TF_FFN

    w_in_b = w_in.astype(BF16)
    w_o_b = w_o.astype(BF16)
    w_up_b = _stage_w_up(w_up, ffp)
    w_down_b = _stage_w_down(w_down, ffp)
    cw_p = _pad_ff_halves(conv_w, ffp, F32)
    cb_p = _pad_ff_halves(conv_b, ffp, F32)[:, None, :]
    row = lambda a: a[:, None, :]
    gmix, gffn = row(norm_mix_g), row(norm_ffn_g)
    lng, lnb = row(sgu_ln_g), row(sgu_ln_b)
    goa, gos = row(out_norm_att_g), row(out_norm_sgu_g)
    gq2 = row(jnp.tile(q_norm_g, (1, 2)))
    gk2 = row(jnp.tile(k_norm_g, (1, 2)))
    b_s_t = jnp.swapaxes(sgu_b, 1, 2)
    wrow = jnp.repeat(jnp.transpose(sgu_w[:, :, :t_new, :t_new], (0, 2, 3, 1)), SGU_HEAD_DIM, axis=-1)
    wrow = wrow.reshape(depth, t_new * t_new, SGU_WIDTH)
    brow = jnp.repeat(jnp.transpose(sgu_b[:, :, :t_new], (0, 2, 1)), SGU_HEAD_DIM, axis=-1)
    sink_rows = jnp.repeat(attn_sinks, t_new, axis=-1)[..., None]
    buf_tm = _pad_ff_halves(jnp.swapaxes(state_ffn_conv, 1, 2), ffp, F32)
    k_cache = cache_k_win.reshape(depth, nbatch, WINDOW, KV_WIDTH)
    v_cache = cache_v_win.reshape(depth, nbatch, WINDOW, KV_WIDTH)

    xp = x_prompt.reshape(batch * seq, d)
    xs = jnp.swapaxes(x_sample, 0, 1).reshape(t_new * nbatch, d)
    kp_l, vp_l, cp_l, ks_l, vs_l, cs_l, sv_l = [], [], [], [], [], [], []
    for l in range(depth):
        q, k, v, u, vs = _proj(l, xp, gmix, w_in_b, gq2, gk2, lng, lnb)
        xp = _mix_prompt(l, attn_sinks, q, k, v, u, vs, sgu_w, b_s_t, goa, gos, xp, w_o_b, batch, seq)
        xp, sa, sg = _ffn_prompt(l, xp, gffn, w_up_b, cw_p, cb_p, w_down_b, seq)
        kp_l.append(k.reshape(batch, seq, KV_HEADS, HEAD_DIM)[:, -WINDOW:])
        vp_l.append(v.reshape(batch, seq, KV_HEADS, HEAD_DIM)[:, -WINDOW:])
        tiles_per_seq = seq // TM_FFN
        cp_l.append(_unpad_ff_halves(sa, sg, d_ff)[tiles_per_seq - 1:batch * tiles_per_seq:tiles_per_seq])

        q, k, v, u, vs = _proj(l, xs, gmix, w_in_b, gq2, gk2, lng, lnb)
        pad_new = ((0, 0), (0, NEW_PAD - t_new), (0, 0))
        k_new = jnp.pad(jnp.swapaxes(k.reshape(t_new, nbatch, KV_WIDTH), 0, 1), pad_new)
        v_new = jnp.pad(jnp.swapaxes(v.reshape(t_new, nbatch, KV_WIDTH), 0, 1), pad_new)
        q5 = jnp.transpose(q.reshape(t_new, nbatch, ATT_HEADS, HEAD_DIM), (1, 2, 0, 3))
        q5 = q5.reshape(nbatch, ATT_HEADS * t_new, HEAD_DIM)
        q_ext = jnp.concatenate([q5, q5], axis=-1)
        o_ext, k_win, v_win = _attn_sample(l, q_ext, k_cache, k_new, v_cache, v_new, sink_rows, t_new)
        o5 = o_ext.reshape(nbatch, GROUP, t_new, KV_HEADS, HEAD_DIM)
        att = jnp.transpose(o5, (2, 0, 3, 1, 4)).reshape(t_new * nbatch, ATT_WIDTH)
        mix = _merge_sample(l, att, u, vs, wrow, brow, goa, gos, nbatch, t_new)
        xs = _out_proj(l, xs, mix, w_o_b)
        xs, sa, sg = _ffn_sample(l, xs, gffn, w_up_b, cw_p, cb_p, w_down_b, buf_tm, nbatch, t_new)
        ks_l.append(k_win.reshape(nbatch, WINDOW, KV_HEADS, HEAD_DIM))
        vs_l.append(v_win.reshape(nbatch, WINDOW, KV_HEADS, HEAD_DIM))
        cs_l.append(jnp.swapaxes(_unpad_ff_halves(sa, sg, d_ff), 0, 1))
        sv_l.append(jnp.swapaxes(vs.reshape(t_new, nbatch, SGU_HEADS, SGU_HEAD_DIM), 0, 1))

    y_prompt = xp.reshape(batch, seq, d)
    y_sample = jnp.swapaxes(xs.reshape(t_new, nbatch, d), 0, 1)
    return (y_prompt, y_sample, jnp.stack(kp_l), jnp.stack(vp_l), jnp.stack(cp_l),
            jnp.stack(ks_l), jnp.stack(vs_l), jnp.stack(cs_l), jnp.stack(sv_l))
```

```python
import functools

import jax
import jax.numpy as jnp
from jax import lax
from jax.experimental import pallas as pl
from jax.experimental.pallas import tpu as pltpu

F32 = jnp.float32
BF16 = jnp.bfloat16

HEAD_DIM = 64
ATT_HEADS = 16
KV_HEADS = 2
GROUP = ATT_HEADS // KV_HEADS
ATT_WIDTH = ATT_HEADS * HEAD_DIM
KV_WIDTH = KV_HEADS * HEAD_DIM
SGU_HEADS = 8
SGU_HEAD_DIM = 128
SGU_WIDTH = SGU_HEADS * SGU_HEAD_DIM
WINDOW = 128
BLOCK = 128
CONV_W = 3
EPS = 1e-6
NEG_INF = -1e30
LANES = 128
MXU_WIDTH = 256
HALO = 16
NEW_PAD = 16
VMEM_LIMIT = 60 * 1024 * 1024

TM_PROJ = 256
MIX_BLOCKS = 2
TM_OUT = 512
TM_FFN = 1024
TF_FFN = 512
BB_ATTN = 32
TR_STAGE = 256

C_Q = ATT_WIDTH
C_K = C_Q + KV_WIDTH
C_V = C_K + KV_WIDTH
C_U = C_V + SGU_WIDTH


def _cparams(n_axes, flags=None):
    return pltpu.CompilerParams(dimension_semantics=("arbitrary",) * n_axes,
                                vmem_limit_bytes=VMEM_LIMIT, flags=flags)


def _layer_spec(l, shape):
    zeros = (0,) * len(shape)
    return pl.BlockSpec((None,) + tuple(shape), lambda *_: (l,) + zeros)


def _rms_rows(x, g):
    y = x * lax.rsqrt(jnp.mean(x * x, axis=-1, keepdims=True) + EPS)
    return y * g


def _rms_head_pairs(z, g2):
    lo = lax.broadcasted_iota(jnp.int32, z.shape, 1) < HEAD_DIM
    z2 = z * z
    s_lo = jnp.sum(jnp.where(lo, z2, 0.0), axis=-1, keepdims=True)
    s_hi = jnp.sum(jnp.where(lo, 0.0, z2), axis=-1, keepdims=True)
    inv = jnp.where(lo, lax.rsqrt(s_lo / HEAD_DIM + EPS), lax.rsqrt(s_hi / HEAD_DIM + EPS))
    return (z * inv) * g2


def _dot(a, b):
    return jnp.dot(a, b, preferred_element_type=F32)


def _dot_t(a, b):
    return lax.dot_general(a, b, (((1,), (1,)), ((), ())), preferred_element_type=F32)


def _proj_body(x_ref, gmix_ref, w_ref, gq_ref, gk_ref, lng_ref, lnb_ref,
               q_ref, k_ref, v_ref, u_ref, vs_ref):
    xn = _rms_rows(x_ref[...], gmix_ref[...]).astype(BF16)
    gv = jax.nn.gelu(_dot(xn, w_ref[:, C_U:]))
    mu = jnp.mean(gv, axis=-1, keepdims=True)
    xc = gv - mu
    y = xc * lax.rsqrt(jnp.mean(xc * xc, axis=-1, keepdims=True) + EPS)
    vs_ref[...] = y * lng_ref[...] + lnb_ref[...]
    u_ref[...] = jax.nn.gelu(_dot(xn, w_ref[:, C_V:C_U]))
    zq = _dot(xn, w_ref[:, 0:C_Q])
    gq = gq_ref[...]
    for c in range(ATT_WIDTH // LANES):
        sl = slice(c * LANES, (c + 1) * LANES)
        q_ref[:, sl] = (_rms_head_pairs(zq[:, sl], gq) * (HEAD_DIM ** -0.5)).astype(q_ref.dtype)
    zkv = _dot(xn, w_ref[:, C_Q:C_V])
    k_ref[...] = _rms_head_pairs(zkv[:, 0:KV_WIDTH], gk_ref[...])
    v_ref[...] = zkv[:, KV_WIDTH:]


def _proj(l, x, gmix, w_in, gq2, gk2, lng, lnb):
    rows, d = x.shape
    tm = min(TM_PROJ, rows)
    in_cols = w_in.shape[-1]
    row_spec = lambda width: pl.BlockSpec((tm, width), lambda i: (i, 0))
    return pl.pallas_call(
        _proj_body,
        grid=(rows // tm,),
        in_specs=[row_spec(d), _layer_spec(l, (1, d)), _layer_spec(l, (d, in_cols)),
                  _layer_spec(l, (1, LANES)), _layer_spec(l, (1, LANES)),
                  _layer_spec(l, (1, SGU_WIDTH)), _layer_spec(l, (1, SGU_WIDTH))],
        out_specs=[row_spec(ATT_WIDTH), row_spec(KV_WIDTH), row_spec(KV_WIDTH),
                   row_spec(SGU_WIDTH), row_spec(SGU_WIDTH)],
        out_shape=[jax.ShapeDtypeStruct((rows, ATT_WIDTH), BF16),
                   jax.ShapeDtypeStruct((rows, KV_WIDTH), F32),
                   jax.ShapeDtypeStruct((rows, KV_WIDTH), F32),
                   jax.ShapeDtypeStruct((rows, SGU_WIDTH), F32),
                   jax.ShapeDtypeStruct((rows, SGU_WIDTH), F32)],
        compiler_params=_cparams(1),
        name="proj",
    )(x, gmix, w_in, gq2, gk2, lng, lnb)


def _alibi_slope(h):
    return 2.0 ** (-8.0 * (h + 1) / ATT_HEADS)


def _softmax_with_sink(s, sink):
    m = jnp.maximum(jnp.max(s, axis=-1, keepdims=True), sink)
    e = jnp.exp(s - m)
    denom = jnp.sum(e, axis=-1, keepdims=True) + jnp.exp(sink - m)
    return e / denom


def _mix_prompt_body(sink_ref, q_ref, kc_ref, kp_ref, vc_ref, vp_ref, u_ref, vs_ref, ws_ref, bs_ref,
                     goa_ref, gos_ref, x_ref, wo_ref, y_ref, bias_scr, wtril_scr, mix0_scr, mix1_scr, *, l, nb):
    s = pl.program_id(0)

    @pl.when(s == 0)
    def _init():
        mix1_scr[...] = jnp.zeros_like(mix1_scr)
        row = lax.broadcasted_iota(jnp.int32, (BLOCK, 2 * BLOCK), 0)
        col = lax.broadcasted_iota(jnp.int32, (BLOCK, 2 * BLOCK), 1)
        dist = BLOCK + row - col
        valid = (dist >= 0) & (dist < WINDOW)
        distf = dist.astype(F32)
        for h in range(ATT_HEADS):
            bias = jnp.where(valid, -_alibi_slope(h) * distf, NEG_INF)
            bias_scr[h] = bias
            bias_scr[ATT_HEADS + h] = jnp.where(col < BLOCK, NEG_INF, bias)
        r = lax.broadcasted_iota(jnp.int32, (BLOCK, BLOCK), 0)
        c = lax.broadcasted_iota(jnp.int32, (BLOCK, BLOCK), 1)
        for h in range(SGU_HEADS):
            wtril_scr[h] = jnp.where(r >= c, ws_ref[h], 0.0).astype(BF16)

    d_out = y_ref.shape[1]
    n_sub = q_ref.shape[0] // BLOCK
    n_slots = n_sub * KV_HEADS * 2
    n_pieces = min(n_slots, d_out // MXU_WIDTH)
    slots_per_piece = n_slots // n_pieces
    piece = d_out // n_pieces
    first_of_seq = (s * n_sub) % nb == 0

    def step(mix_cur, mix_prev):
        lo = lax.broadcasted_iota(jnp.int32, (2 * BLOCK, LANES), 1) < HEAD_DIM
        mix_p = mix_prev[...]
        for t in range(n_sub):
            rows = slice(t * BLOCK, (t + 1) * BLOCK)
            if t == 0:
                bias_base = jnp.where(first_of_seq, ATT_HEADS, 0)
                k_prev, v_prev = kp_ref[...], vp_ref[...]
            else:
                bias_base = 0
                k_prev, v_prev = kc_ref[(t - 1) * BLOCK:t * BLOCK, :], vc_ref[(t - 1) * BLOCK:t * BLOCK, :]
            kband = jnp.concatenate([k_prev, kc_ref[rows, :]], axis=0)
            vband = jnp.concatenate([v_prev, vc_ref[rows, :]], axis=0)
            kroll = pltpu.roll(kband, HEAD_DIM, 1)
            vroll = pltpu.roll(vband, HEAD_DIM, 1)
            q = q_ref[rows, :]
            att_cols = []
            for g in range(KV_HEADS):
                k_src_lo, k_src_hi = (kband, kroll) if g == 0 else (kroll, kband)
                v_src_lo, v_src_hi = (vband, vroll) if g == 0 else (vroll, vband)
                k_par = (jnp.where(lo, k_src_lo, 0.0).astype(BF16), jnp.where(lo, 0.0, k_src_hi).astype(BF16))
                v_par = (jnp.where(lo, v_src_lo, 0.0).astype(BF16), jnp.where(lo, 0.0, v_src_hi).astype(BF16))
                pairs = GROUP // 2
                qp = jnp.concatenate(
                    [q[:, (g * pairs + p) * LANES:(g * pairs + p + 1) * LANES] for p in range(pairs)], axis=0)
                o_pair = None
                for par in range(2):
                    s_all = _dot_t(qp, k_par[par])
                    probs = []
                    for p in range(pairs):
                        h = g * GROUP + 2 * p + par
                        sc = s_all[p * BLOCK:(p + 1) * BLOCK, :] + bias_scr[bias_base + h]
                        probs.append(_softmax_with_sink(sc, sink_ref[l, h]).astype(BF16))
                    slot = (t * KV_HEADS + g) * 2 + par
                    if slot % slots_per_piece == 0:
                        pc = slot // slots_per_piece
                        cols = slice(pc * piece, (pc + 1) * piece)
                        y_ref[:, cols] = x_ref[:, cols] + _dot(mix_p, wo_ref[:, cols])
                    o = _dot(jnp.concatenate(probs, axis=0), v_par[par])
                    o_pair = o if o_pair is None else o_pair + o
                for p in range(pairs):
                    att_cols.append(o_pair[p * BLOCK:(p + 1) * BLOCK, :])
            att = jnp.concatenate(att_cols, axis=-1)
            mix_cur[rows, 0:ATT_WIDTH] = _rms_rows(att, goa_ref[...]).astype(BF16)

            vs = vs_ref[rows, :].astype(BF16)
            bs = bs_ref[...]
            sgu_cols = []
            for h in range(SGU_HEADS):
                sl = slice(h * SGU_HEAD_DIM, (h + 1) * SGU_HEAD_DIM)
                mixed = _dot(wtril_scr[h], vs[:, sl]) + bs[:, h:h + 1]
                sgu_cols.append(u_ref[rows, sl] * mixed)
            sgu = jnp.concatenate(sgu_cols, axis=-1)
            mix_cur[rows, ATT_WIDTH:] = _rms_rows(sgu, gos_ref[...]).astype(BF16)

    @pl.when(s % 2 == 0)
    def _even():
        step(mix0_scr, mix1_scr)

    @pl.when(s % 2 == 1)
    def _odd():
        step(mix1_scr, mix0_scr)


def _mix_prompt(l, sinks, q, k, v, u, vs, w_s, b_s_t, goa, gos, x, w_o, batch, seq):
    nb = seq // BLOCK
    n_sub = MIX_BLOCKS
    assert nb % n_sub == 0
    rows = n_sub * BLOCK
    n_steps = batch * nb // n_sub
    d = x.shape[1]
    cur_i = lambda s: jnp.minimum(s, n_steps - 1)
    lag = lambda s: jnp.maximum(s - 1, 0)
    cur = lambda width: pl.BlockSpec((rows, width), lambda s: (cur_i(s), 0))
    prev = lambda width: pl.BlockSpec(
        (BLOCK, width), lambda s: (cur_i(s) * n_sub - jnp.where((s * n_sub) % nb == 0, 0, 1), 0))
    lagged = pl.BlockSpec((rows, d), lambda s: (lag(s), 0))
    return pl.pallas_call(
        functools.partial(_mix_prompt_body, l=l, nb=nb),
        grid=(n_steps + 1,),
        in_specs=[pl.BlockSpec(memory_space=pltpu.SMEM),
                  cur(ATT_WIDTH), cur(KV_WIDTH), prev(KV_WIDTH), cur(KV_WIDTH), prev(KV_WIDTH),
                  cur(SGU_WIDTH), cur(SGU_WIDTH),
                  _layer_spec(l, (SGU_HEADS, BLOCK, BLOCK)), _layer_spec(l, (BLOCK, SGU_HEADS)),
                  _layer_spec(l, (1, ATT_WIDTH)), _layer_spec(l, (1, SGU_WIDTH)),
                  lagged, _layer_spec(l, w_o.shape[1:])],
        out_specs=lagged,
        out_shape=jax.ShapeDtypeStruct(x.shape, F32),
        scratch_shapes=[pltpu.VMEM((2 * ATT_HEADS, BLOCK, 2 * BLOCK), F32),
                        pltpu.VMEM((SGU_HEADS, BLOCK, BLOCK), BF16),
                        pltpu.VMEM((rows, ATT_WIDTH + SGU_WIDTH), BF16),
                        pltpu.VMEM((rows, ATT_WIDTH + SGU_WIDTH), BF16)],
        compiler_params=_cparams(1),
        name="mix_prompt",
    )(sinks, q, k, k, v, v, u, vs, w_s, b_s_t, goa, gos, x, w_o)


def _attn_sample_body(q_ref, kc_ref, kn_ref, vc_ref, vn_ref, sink_ref, o_ref, kw_ref, vw_ref, *, t_new):
    rows = KV_HEADS * GROUP * t_new
    keys = WINDOW + NEW_PAD
    t_shift = t_new.bit_length() - 1
    assert t_new == 1 << t_shift
    r = lax.broadcasted_iota(jnp.int32, (rows, 1), 0)
    t = jnp.bitwise_and(r, t_new - 1)
    head = lax.shift_right_logical(r, t_shift)
    j = lax.broadcasted_iota(jnp.int32, (rows, keys), 1)
    dist = WINDOW + t - j
    valid = (dist >= 0) & (dist < WINDOW)
    slope = jnp.exp2(-8.0 * (head + 1).astype(F32) / ATT_HEADS)
    bias = jnp.where(valid, -slope * dist.astype(F32), NEG_INF)
    lane_lo = lax.broadcasted_iota(jnp.int32, (rows, LANES), 1) < HEAD_DIM
    row_kv0 = lax.broadcasted_iota(jnp.int32, (rows, LANES), 0) < GROUP * t_new
    q_keep = jnp.where(lane_lo == row_kv0, 1.0, 0.0).astype(BF16)

    kx = jnp.concatenate([kc_ref[...], kn_ref[...]], axis=1)
    vx = jnp.concatenate([vc_ref[...], vn_ref[...]], axis=1)
    for w_ref, c_ref, n_ref in ((kw_ref, kc_ref, kn_ref), (vw_ref, vc_ref, vn_ref)):
        w_ref[:, 0:WINDOW - t_new, :] = c_ref[:, t_new:WINDOW, :]
        w_ref[:, WINDOW - t_new:WINDOW, :] = n_ref[:, 0:t_new, :]
    q = q_ref[...] * q_keep[None]
    s = jnp.einsum('brd,bjd->brj', q, kx.astype(BF16), preferred_element_type=F32) + bias[None]
    p = _softmax_with_sink(s, sink_ref[...][None]).astype(BF16)
    o = jnp.einsum('brj,bjd->brd', p, vx.astype(BF16), preferred_element_type=F32)
    half = GROUP * t_new
    out_lo = lax.broadcasted_iota(jnp.int32, (o.shape[0], half, LANES), 2) < HEAD_DIM
    o_ref[...] = jnp.where(out_lo, o[:, :half], o[:, half:])


def _attn_sample(l, q_ext, k_cache, k_new, v_cache, v_new, sink_rows, t_new):
    nbatch = q_ext.shape[0]
    bb = BB_ATTN
    rows = KV_HEADS * GROUP * t_new
    b3 = lambda shape: pl.BlockSpec((bb,) + shape, lambda i: (i,) + (0,) * len(shape))
    cache = pl.BlockSpec((None, bb, WINDOW, KV_WIDTH), lambda i: (l, i, 0, 0))
    return pl.pallas_call(
        functools.partial(_attn_sample_body, t_new=t_new),
        grid=(nbatch // bb,),
        in_specs=[b3((rows, LANES)), cache, b3((NEW_PAD, KV_WIDTH)), cache, b3((NEW_PAD, KV_WIDTH)),
                  _layer_spec(l, (rows, 1))],
        out_specs=[b3((rows // KV_HEADS, LANES)), b3((WINDOW, KV_WIDTH)), b3((WINDOW, KV_WIDTH))],
        out_shape=[jax.ShapeDtypeStruct((nbatch, rows // KV_HEADS, LANES), F32),
                   jax.ShapeDtypeStruct((nbatch, WINDOW, KV_WIDTH), F32),
                   jax.ShapeDtypeStruct((nbatch, WINDOW, KV_WIDTH), F32)],
        compiler_params=_cparams(1),
        name="attn_sample",
    )(q_ext, k_cache, k_new, v_cache, v_new, sink_rows)


def _merge_sample_body(att_ref, u_ref, vs_ref, wrow_ref, brow_ref, goa_ref, gos_ref, mix_ref, *, nbatch, t_new):
    mix_ref[:, 0:ATT_WIDTH] = _rms_rows(att_ref[...], goa_ref[...]).astype(mix_ref.dtype)
    for t in range(t_new):
        mixed = None
        for s in range(t + 1):
            term = wrow_ref[t * t_new + s:t * t_new + s + 1, :] * vs_ref[s * nbatch:(s + 1) * nbatch, :]
            mixed = term if mixed is None else mixed + term
        mixed = mixed + brow_ref[t:t + 1, :]
        sgu = u_ref[t * nbatch:(t + 1) * nbatch, :] * mixed
        mix_ref[t * nbatch:(t + 1) * nbatch, ATT_WIDTH:] = _rms_rows(sgu, gos_ref[...]).astype(mix_ref.dtype)


def _merge_sample(l, att, u, vs, wrow, brow, goa, gos, nbatch, t_new):
    rows = att.shape[0]
    whole = lambda a: pl.BlockSpec(a.shape, lambda i: (0,) * a.ndim)
    return pl.pallas_call(
        functools.partial(_merge_sample_body, nbatch=nbatch, t_new=t_new),
        grid=(1,),
        in_specs=[whole(att), whole(u), whole(vs), _layer_spec(l, wrow.shape[1:]), _layer_spec(l, brow.shape[1:]),
                  _layer_spec(l, (1, ATT_WIDTH)), _layer_spec(l, (1, SGU_WIDTH))],
        out_specs=pl.BlockSpec((rows, ATT_WIDTH + SGU_WIDTH), lambda i: (0, 0)),
        out_shape=jax.ShapeDtypeStruct((rows, ATT_WIDTH + SGU_WIDTH), BF16),
        compiler_params=_cparams(1),
        name="merge_sample",
    )(att, u, vs, wrow, brow, goa, gos)


def _out_proj_body(x_ref, mix_ref, w_ref, y_ref):
    y_ref[...] = x_ref[...] + _dot(mix_ref[...], w_ref[...])


def _out_proj(l, x, mix, w_o):
    rows, d = x.shape
    tm = min(TM_OUT, rows)
    return pl.pallas_call(
        _out_proj_body,
        grid=(rows // tm,),
        in_specs=[pl.BlockSpec((tm, d), lambda i: (i, 0)),
                  pl.BlockSpec((tm, mix.shape[1]), lambda i: (i, 0)),
                  _layer_spec(l, w_o.shape[1:])],
        out_specs=pl.BlockSpec((tm, d), lambda i: (i, 0)),
        out_shape=jax.ShapeDtypeStruct((rows, d), F32),
        compiler_params=_cparams(1),
        name="out_proj",
    )(x, mix, w_o)


def _silu_gate(hc_a, hc_g):
    return (jax.nn.silu(hc_g) * hc_a).astype(BF16)


def _ffn_prompt_body(x_ref, halo_ref, g_ref, wa_ref, wg_ref, cwa_ref, cwg_ref, cba_ref, cbg_ref, wd_ref,
                     y_ref, sa_ref, sg_ref, xn_scr, ha_scr, hg_scr, *, tm, tf, tiles_per_seq):
    i = pl.program_id(0)
    j = pl.program_id(1)

    @pl.when(j == 0)
    def _prologue():
        x = x_ref[...]
        xn_scr[HALO:, :] = _rms_rows(x, g_ref[...]).astype(BF16)
        keep = jnp.where((i % tiles_per_seq) == 0, 0.0, 1.0)
        xn_scr[0:HALO, :] = (_rms_rows(halo_ref[...], g_ref[...]) * keep).astype(BF16)
        y_ref[...] = x

    xe = xn_scr[...]

    def conv(h_scr, cw_ref, cb_ref, cs, r0, nr):
        hc = cb_ref[:, cs] + cw_ref[0:1, cs] * h_scr[pl.ds(HALO - 2 + r0, nr), cs]
        hc = hc + cw_ref[1:2, cs] * h_scr[pl.ds(HALO - 1 + r0, nr), cs]
        return hc + cw_ref[2:3, cs] * h_scr[pl.ds(HALO + r0, nr), cs]

    n_chunks = tf // MXU_WIDTH
    for c in range(n_chunks):
        cs = slice(c * MXU_WIDTH, (c + 1) * MXU_WIDTH)
        ha_scr[:, cs] = _dot(xe, wa_ref[:, cs])
        hg_scr[:, cs] = _dot(xe, wg_ref[:, cs])
    for c in range(n_chunks):
        cs = slice(c * MXU_WIDTH, (c + 1) * MXU_WIDTH)
        n_row = 2 if c == n_chunks - 1 else 1
        nr = tm // n_row
        for r in range(n_row):
            act = _silu_gate(conv(ha_scr, cwa_ref, cba_ref, cs, r * nr, nr),
                             conv(hg_scr, cwg_ref, cbg_ref, cs, r * nr, nr))
            y_ref[r * nr:(r + 1) * nr, :] += _dot(act, wd_ref[cs, :])
    sa_ref[...] = ha_scr[pl.ds(HALO + tm - (CONV_W - 1), CONV_W - 1), :]
    sg_ref[...] = hg_scr[pl.ds(HALO + tm - (CONV_W - 1), CONV_W - 1), :]


def _ffn_prompt(l, x, g, w_up, cw, cb, w_down, seq):
    rows, d = x.shape
    tm, tf = TM_FFN, TF_FFN
    ffp = w_down.shape[1]
    nj = ffp // tf
    tiles_per_seq = seq // tm
    halo_blocks = tm // HALO
    lyr = lambda shape, imap: pl.BlockSpec((None,) + shape, lambda i, j: (l,) + imap(i, j))
    return pl.pallas_call(
        functools.partial(_ffn_prompt_body, tm=tm, tf=tf, tiles_per_seq=tiles_per_seq),
        grid=(rows // tm, nj),
        in_specs=[pl.BlockSpec((tm, d), lambda i, j: (i, 0)),
                  pl.BlockSpec((HALO, d), lambda i, j: (jnp.maximum(i * halo_blocks - 1, 0), 0)),
                  _layer_spec(l, (1, d)),
                  lyr((d, tf), lambda i, j: (0, j)),
                  lyr((d, tf), lambda i, j: (0, j + nj)),
                  lyr((CONV_W, tf), lambda i, j: (0, j)),
                  lyr((CONV_W, tf), lambda i, j: (0, j + nj)),
                  lyr((1, tf), lambda i, j: (0, j)),
                  lyr((1, tf), lambda i, j: (0, j + nj)),
                  lyr((tf, d), lambda i, j: (j, 0))],
        out_specs=[pl.BlockSpec((tm, d), lambda i, j: (i, 0)),
                   pl.BlockSpec((None, CONV_W - 1, tf), lambda i, j: (i, 0, j)),
                   pl.BlockSpec((None, CONV_W - 1, tf), lambda i, j: (i, 0, j))],
        out_shape=[jax.ShapeDtypeStruct((rows, d), F32),
                   jax.ShapeDtypeStruct((rows // tm, CONV_W - 1, ffp), F32),
                   jax.ShapeDtypeStruct((rows // tm, CONV_W - 1, ffp), F32)],
        scratch_shapes=[pltpu.VMEM((HALO + tm, d), BF16),
                        pltpu.VMEM((HALO + tm, tf), F32),
                        pltpu.VMEM((HALO + tm, tf), F32)],
        compiler_params=_cparams(2),
        name="ffn_prompt",
    )(x, x, g, w_up, w_up, cw, cw, cb, cb, w_down)


def _ffn_sample_body(x_ref, g_ref, wa_ref, wg_ref, cwa_ref, cwg_ref, cba_ref, cbg_ref, wd_ref, bufa_ref, bufg_ref,
                     y_ref, sa_ref, sg_ref, xn_scr, *, nbatch, t_new):
    j = pl.program_id(0)

    @pl.when(j == 0)
    def _prologue():
        x = x_ref[...]
        xn_scr[...] = _rms_rows(x, g_ref[...]).astype(BF16)
        y_ref[...] = x

    xn = xn_scr[...]
    ha = _dot(xn, wa_ref[...])
    hg = _dot(xn, wg_ref[...])

    def conv(h, buf_ref, cw_ref, cb_ref):
        hp = [buf_ref[k] for k in range(CONV_W - 1)] + [h[t * nbatch:(t + 1) * nbatch, :] for t in range(t_new)]
        out = []
        for t in range(t_new):
            hc = cb_ref[...] + cw_ref[0:1, :] * hp[t]
            for k in range(1, CONV_W):
                hc = hc + cw_ref[k:k + 1, :] * hp[t + k]
            out.append(hc)
        return jnp.concatenate(out, axis=0)

    act = _silu_gate(conv(ha, bufa_ref, cwa_ref, cba_ref), conv(hg, bufg_ref, cwg_ref, cbg_ref))
    y_ref[...] += _dot(act, wd_ref[...])
    for k in range(CONV_W - 1):
        t = t_new - (CONV_W - 1) + k
        sa_ref[k] = ha[t * nbatch:(t + 1) * nbatch, :]
        sg_ref[k] = hg[t * nbatch:(t + 1) * nbatch, :]


def _ffn_sample(l, x, g, w_up, cw, cb, w_down, buf, nbatch, t_new):
    rows, d = x.shape
    tf = TF_FFN
    ffp = w_down.shape[1]
    nj = ffp // tf
    lyr = lambda shape, imap: pl.BlockSpec((None,) + shape, lambda j: (l,) + imap(j))
    state = lambda off: pl.BlockSpec((CONV_W - 1, nbatch, tf), lambda j: (0, 0, j + off))
    return pl.pallas_call(
        functools.partial(_ffn_sample_body, nbatch=nbatch, t_new=t_new),
        grid=(nj,),
        in_specs=[pl.BlockSpec((rows, d), lambda j: (0, 0)),
                  _layer_spec(l, (1, d)),
                  lyr((d, tf), lambda j: (0, j)),
                  lyr((d, tf), lambda j: (0, j + nj)),
                  lyr((CONV_W, tf), lambda j: (0, j)),
                  lyr((CONV_W, tf), lambda j: (0, j + nj)),
                  lyr((1, tf), lambda j: (0, j)),
                  lyr((1, tf), lambda j: (0, j + nj)),
                  lyr((tf, d), lambda j: (j, 0)),
                  lyr((CONV_W - 1, nbatch, tf), lambda j: (0, 0, j)),
                  lyr((CONV_W - 1, nbatch, tf), lambda j: (0, 0, j + nj))],
        out_specs=[pl.BlockSpec((rows, d), lambda j: (0, 0)), state(0), state(0)],
        out_shape=[jax.ShapeDtypeStruct((rows, d), F32),
                   jax.ShapeDtypeStruct((CONV_W - 1, nbatch, ffp), F32),
                   jax.ShapeDtypeStruct((CONV_W - 1, nbatch, ffp), F32)],
        scratch_shapes=[pltpu.VMEM((rows, d), BF16)],
        compiler_params=_cparams(1),
        name="ffn_sample",
    )(x, g, w_up, w_up, cw, cw, cb, cb, w_down, buf, buf)


def _stage_body(w_ref, o_ref, *, axis):
    n = w_ref.shape[axis]
    if axis == 0:
        o_ref[0:n, :] = w_ref[...].astype(o_ref.dtype)
        o_ref[n:, :] = jnp.zeros((o_ref.shape[0] - n, o_ref.shape[1]), o_ref.dtype)
    else:
        o_ref[:, 0:n] = w_ref[...].astype(o_ref.dtype)
        o_ref[:, n:] = jnp.zeros((o_ref.shape[0], o_ref.shape[1] - n), o_ref.dtype)


def _stage_w_up(w_up, ffp):
    depth, d, two_ff = w_up.shape
    d_ff = two_ff // 2
    tr = TR_STAGE
    return pl.pallas_call(
        functools.partial(_stage_body, axis=1),
        grid=(depth, d // tr, 2),
        in_specs=[pl.BlockSpec((None, tr, d_ff), lambda l, r, h: (l, r, h))],
        out_specs=pl.BlockSpec((None, tr, ffp), lambda l, r, h: (l, r, h)),
        out_shape=jax.ShapeDtypeStruct((depth, d, 2 * ffp), BF16),
        compiler_params=_cparams(3),
        name="stage_w_up",
    )(w_up)


def _stage_w_down(w_down, ffp):
    depth, d_ff, d = w_down.shape
    tc = TR_STAGE
    return pl.pallas_call(
        functools.partial(_stage_body, axis=0),
        grid=(depth, d // tc),
        in_specs=[pl.BlockSpec((None, d_ff, tc), lambda l, c: (l, 0, c))],
        out_specs=pl.BlockSpec((None, ffp, tc), lambda l, c: (l, 0, c)),
        out_shape=jax.ShapeDtypeStruct((depth, ffp, d), BF16),
        compiler_params=_cparams(2),
        name="stage_w_down",
    )(w_down)


def _pad_ff_halves(a, ffp, dtype):
    d_ff = a.shape[-1] // 2
    z = jnp.zeros(a.shape[:-1] + (ffp - d_ff,), dtype)
    return jnp.concatenate([a[..., :d_ff].astype(dtype), z, a[..., d_ff:].astype(dtype), z], axis=-1)


def _unpad_ff_halves(sa, sg, d_ff):
    return jnp.concatenate([sa[..., :d_ff], sg[..., :d_ff]], axis=-1)


def kernel(x_prompt, x_sample, cache_k_win, cache_v_win, state_ffn_conv, norm_mix_g, w_in, q_norm_g, k_norm_g,
           attn_sinks, sgu_ln_g, sgu_ln_b, sgu_w, sgu_b, out_norm_att_g, out_norm_sgu_g, w_o, norm_ffn_g,
           w_up, conv_w, conv_b, w_down):
    batch, seq, d = x_prompt.shape
    nbatch, t_new, _ = x_sample.shape
    depth = w_in.shape[0]
    d_ff = w_down.shape[1]
    ffp = -(-d_ff // TF_FFN) * TF_FFN

    w_in_b = w_in.astype(BF16)
    w_o_b = w_o.astype(BF16)
    w_up_b = _stage_w_up(w_up, ffp)
    w_down_b = _stage_w_down(w_down, ffp)
    cw_p = _pad_ff_halves(conv_w, ffp, F32)
    cb_p = _pad_ff_halves(conv_b, ffp, F32)[:, None, :]
    row = lambda a: a[:, None, :]
    gmix, gffn = row(norm_mix_g), row(norm_ffn_g)
    lng, lnb = row(sgu_ln_g), row(sgu_ln_b)
    goa, gos = row(out_norm_att_g), row(out_norm_sgu_g)
    gq2 = row(jnp.tile(q_norm_g, (1, 2)))
    gk2 = row(jnp.tile(k_norm_g, (1, 2)))
    b_s_t = jnp.swapaxes(sgu_b, 1, 2)
    wrow = jnp.repeat(jnp.transpose(sgu_w[:, :, :t_new, :t_new], (0, 2, 3, 1)), SGU_HEAD_DIM, axis=-1)
    wrow = wrow.reshape(depth, t_new * t_new, SGU_WIDTH)
    brow = jnp.repeat(jnp.transpose(sgu_b[:, :, :t_new], (0, 2, 1)), SGU_HEAD_DIM, axis=-1)
    sink_rows = jnp.repeat(attn_sinks, t_new, axis=-1)[..., None]
    buf_tm = _pad_ff_halves(jnp.swapaxes(state_ffn_conv, 1, 2), ffp, F32)
    k_cache = cache_k_win.reshape(depth, nbatch, WINDOW, KV_WIDTH)
    v_cache = cache_v_win.reshape(depth, nbatch, WINDOW, KV_WIDTH)

    xp = x_prompt.reshape(batch * seq, d)
    xs = jnp.swapaxes(x_sample, 0, 1).reshape(t_new * nbatch, d)
    kp_l, vp_l, cp_l, ks_l, vs_l, cs_l, sv_l = [], [], [], [], [], [], []
    for l in range(depth):
        q, k, v, u, vs = _proj(l, xp, gmix, w_in_b, gq2, gk2, lng, lnb)
        xp = _mix_prompt(l, attn_sinks, q, k, v, u, vs, sgu_w, b_s_t, goa, gos, xp, w_o_b, batch, seq)
        xp, sa, sg = _ffn_prompt(l, xp, gffn, w_up_b, cw_p, cb_p, w_down_b, seq)
        kp_l.append(k.reshape(batch, seq, KV_HEADS, HEAD_DIM)[:, -WINDOW:])
        vp_l.append(v.reshape(batch, seq, KV_HEADS, HEAD_DIM)[:, -WINDOW:])
        tiles_per_seq = seq // TM_FFN
        cp_l.append(_unpad_ff_halves(sa, sg, d_ff)[tiles_per_seq - 1:batch * tiles_per_seq:tiles_per_seq])

        q, k, v, u, vs = _proj(l, xs, gmix, w_in_b, gq2, gk2, lng, lnb)
        pad_new = ((0, 0), (0, NEW_PAD - t_new), (0, 0))
        k_new = jnp.pad(jnp.swapaxes(k.reshape(t_new, nbatch, KV_WIDTH), 0, 1), pad_new)
        v_new = jnp.pad(jnp.swapaxes(v.reshape(t_new, nbatch, KV_WIDTH), 0, 1), pad_new)
        q5 = jnp.transpose(q.reshape(t_new, nbatch, ATT_HEADS, HEAD_DIM), (1, 2, 0, 3))
        q5 = q5.reshape(nbatch, ATT_HEADS * t_new, HEAD_DIM)
        q_ext = jnp.concatenate([q5, q5], axis=-1)
        o_ext, k_win, v_win = _attn_sample(l, q_ext, k_cache, k_new, v_cache, v_new, sink_rows, t_new)
        o5 = o_ext.reshape(nbatch, GROUP, t_new, KV_HEADS, HEAD_DIM)
        att = jnp.transpose(o5, (2, 0, 3, 1, 4)).reshape(t_new * nbatch, ATT_WIDTH)
        mix = _merge_sample(l, att, u, vs, wrow, brow, goa, gos, nbatch, t_new)
        xs = _out_proj(l, xs, mix, w_o_b)
        xs, sa, sg = _ffn_sample(l, xs, gffn, w_up_b, cw_p, cb_p, w_down_b, buf_tm, nbatch, t_new)
        ks_l.append(k_win.reshape(nbatch, WINDOW, KV_HEADS, HEAD_DIM))
        vs_l.append(v_win.reshape(nbatch, WINDOW, KV_HEADS, HEAD_DIM))
        cs_l.append(jnp.swapaxes(_unpad_ff_halves(sa, sg, d_ff), 0, 1))
        sv_l.append(jnp.swapaxes(vs.reshape(t_new, nbatch, SGU_HEADS, SGU_HEAD_DIM), 0, 1))

    y_prompt = xp.reshape(batch, seq, d)
    y_sample = jnp.swapaxes(xs.reshape(t_new, nbatch, d), 0, 1)
    return (y_prompt, y_sample, jnp.stack(kp_l), jnp.stack(vp_l), jnp.stack(cp_l),
            jnp.stack(ks_l), jnp.stack(vs_l), jnp.stack(cs_l), jnp.stack(sv_l))
```

```python
import functools

import jax
import jax.numpy as jnp
from jax import lax
from jax.experimental import pallas as pl
from jax.experimental.pallas import tpu as pltpu

F32 = jnp.float32
BF16 = jnp.bfloat16

HEAD_DIM = 64
ATT_HEADS = 16
KV_HEADS = 2
GROUP = ATT_HEADS // KV_HEADS
ATT_WIDTH = ATT_HEADS * HEAD_DIM
KV_WIDTH = KV_HEADS * HEAD_DIM
SGU_HEADS = 8
SGU_HEAD_DIM = 128
SGU_WIDTH = SGU_HEADS * SGU_HEAD_DIM
WINDOW = 128
BLOCK = 128
CONV_W = 3
EPS = 1e-6
NEG_INF = -1e30
LANES = 128
MXU_WIDTH = 256
HALO = 16
NEW_PAD = 16
VMEM_LIMIT = 60 * 1024 * 1024

TM_PROJ = 256
MIX_BLOCKS = 2
TM_OUT = 512
TM_FFN = 1024
TF_FFN = 512
BB_ATTN = 32
TR_STAGE = 512

C_Q = ATT_WIDTH
C_K = C_Q + KV_WIDTH
C_V = C_K + KV_WIDTH
C_U = C_V + SGU_WIDTH


def _cparams(n_axes, flags=None):
    return pltpu.CompilerParams(dimension_semantics=("arbitrary",) * n_axes,
                                vmem_limit_bytes=VMEM_LIMIT, flags=flags)


def _layer_spec(l, shape):
    zeros = (0,) * len(shape)
    return pl.BlockSpec((None,) + tuple(shape), lambda *_: (l,) + zeros)


def _rms_rows(x, g):
    y = x * lax.rsqrt(jnp.mean(x * x, axis=-1, keepdims=True) + EPS)
    return y * g


def _rms_head_pairs(z, g2):
    lo = lax.broadcasted_iota(jnp.int32, z.shape, 1) < HEAD_DIM
    z2 = z * z
    s_lo = jnp.sum(jnp.where(lo, z2, 0.0), axis=-1, keepdims=True)
    s_hi = jnp.sum(jnp.where(lo, 0.0, z2), axis=-1, keepdims=True)
    inv = jnp.where(lo, lax.rsqrt(s_lo / HEAD_DIM + EPS), lax.rsqrt(s_hi / HEAD_DIM + EPS))
    return (z * inv) * g2


def _dot(a, b):
    return jnp.dot(a, b, preferred_element_type=F32)


def _dot_t(a, b):
    return lax.dot_general(a, b, (((1,), (1,)), ((), ())), preferred_element_type=F32)


def _proj_body(x_ref, gmix_ref, w_ref, gq_ref, gk_ref, lng_ref, lnb_ref,
               q_ref, k_ref, v_ref, u_ref, vs_ref):
    xn = _rms_rows(x_ref[...], gmix_ref[...]).astype(BF16)
    gv = jax.nn.gelu(_dot(xn, w_ref[:, C_U:]))
    mu = jnp.mean(gv, axis=-1, keepdims=True)
    xc = gv - mu
    y = xc * lax.rsqrt(jnp.mean(xc * xc, axis=-1, keepdims=True) + EPS)
    vs_ref[...] = y * lng_ref[...] + lnb_ref[...]
    zq = _dot(xn, w_ref[:, 0:C_Q])
    gq = gq_ref[...]
    for c in range(ATT_WIDTH // LANES):
        sl = slice(c * LANES, (c + 1) * LANES)
        q_ref[:, sl] = (_rms_head_pairs(zq[:, sl], gq) * (HEAD_DIM ** -0.5)).astype(q_ref.dtype)
    u_ref[...] = jax.nn.gelu(_dot(xn, w_ref[:, C_V:C_U]))
    zkv = _dot(xn, w_ref[:, C_Q:C_V])
    k_ref[...] = _rms_head_pairs(zkv[:, 0:KV_WIDTH], gk_ref[...])
    v_ref[...] = zkv[:, KV_WIDTH:]


def _proj(l, x, gmix, w_in, gq2, gk2, lng, lnb):
    rows, d = x.shape
    tm = min(TM_PROJ, rows)
    in_cols = w_in.shape[-1]
    row_spec = lambda width: pl.BlockSpec((tm, width), lambda i: (i, 0))
    return pl.pallas_call(
        _proj_body,
        grid=(rows // tm,),
        in_specs=[row_spec(d), _layer_spec(l, (1, d)), _layer_spec(l, (d, in_cols)),
                  _layer_spec(l, (1, LANES)), _layer_spec(l, (1, LANES)),
                  _layer_spec(l, (1, SGU_WIDTH)), _layer_spec(l, (1, SGU_WIDTH))],
        out_specs=[row_spec(ATT_WIDTH), row_spec(KV_WIDTH), row_spec(KV_WIDTH),
                   row_spec(SGU_WIDTH), row_spec(SGU_WIDTH)],
        out_shape=[jax.ShapeDtypeStruct((rows, ATT_WIDTH), BF16),
                   jax.ShapeDtypeStruct((rows, KV_WIDTH), F32),
                   jax.ShapeDtypeStruct((rows, KV_WIDTH), F32),
                   jax.ShapeDtypeStruct((rows, SGU_WIDTH), F32),
                   jax.ShapeDtypeStruct((rows, SGU_WIDTH), F32)],
        compiler_params=_cparams(1),
        name="proj",
    )(x, gmix, w_in, gq2, gk2, lng, lnb)


def _alibi_slope(h):
    return 2.0 ** (-8.0 * (h + 1) / ATT_HEADS)


def _softmax_with_sink(s, sink):
    m = jnp.maximum(jnp.max(s, axis=-1, keepdims=True), sink)
    e = jnp.exp(s - m)
    denom = jnp.sum(e, axis=-1, keepdims=True) + jnp.exp(sink - m)
    return e / denom


def _mix_prompt_body(sink_ref, q_ref, kc_ref, kp_ref, vc_ref, vp_ref, u_ref, vs_ref, ws_ref, bs_ref,
                     goa_ref, gos_ref, x_ref, wo_ref, y_ref, bias_scr, wtril_scr, mix0_scr, mix1_scr, *, l, nb):
    s = pl.program_id(0)

    @pl.when(s == 0)
    def _init():
        mix1_scr[...] = jnp.zeros_like(mix1_scr)
        row = lax.broadcasted_iota(jnp.int32, (BLOCK, 2 * BLOCK), 0)
        col = lax.broadcasted_iota(jnp.int32, (BLOCK, 2 * BLOCK), 1)
        dist = BLOCK + row - col
        valid = (dist >= 0) & (dist < WINDOW)
        distf = dist.astype(F32)
        for h in range(ATT_HEADS):
            bias = jnp.where(valid, -_alibi_slope(h) * distf, NEG_INF)
            bias_scr[h] = bias
            bias_scr[ATT_HEADS + h] = jnp.where(col < BLOCK, NEG_INF, bias)
        r = lax.broadcasted_iota(jnp.int32, (BLOCK, BLOCK), 0)
        c = lax.broadcasted_iota(jnp.int32, (BLOCK, BLOCK), 1)
        for h in range(SGU_HEADS):
            wtril_scr[h] = jnp.where(r >= c, ws_ref[h], 0.0).astype(BF16)

    d_out = y_ref.shape[1]
    n_sub = q_ref.shape[0] // BLOCK
    n_slots = n_sub * KV_HEADS * 2
    n_pieces = min(n_slots, d_out // MXU_WIDTH)
    slots_per_piece = n_slots // n_pieces
    piece = d_out // n_pieces
    first_of_seq = (s * n_sub) % nb == 0

    def step(mix_cur, mix_prev):
        lo = lax.broadcasted_iota(jnp.int32, (2 * BLOCK, LANES), 1) < HEAD_DIM
        mix_p = mix_prev[...]
        for t in range(n_sub):
            rows = slice(t * BLOCK, (t + 1) * BLOCK)
            if t == 0:
                bias_base = jnp.where(first_of_seq, ATT_HEADS, 0)
                k_prev, v_prev = kp_ref[...], vp_ref[...]
            else:
                bias_base = 0
                k_prev, v_prev = kc_ref[(t - 1) * BLOCK:t * BLOCK, :], vc_ref[(t - 1) * BLOCK:t * BLOCK, :]
            kband = jnp.concatenate([k_prev, kc_ref[rows, :]], axis=0)
            vband = jnp.concatenate([v_prev, vc_ref[rows, :]], axis=0)
            kroll = pltpu.roll(kband, HEAD_DIM, 1)
            vroll = pltpu.roll(vband, HEAD_DIM, 1)
            q = q_ref[rows, :]
            att_cols = []
            for g in range(KV_HEADS):
                k_src_lo, k_src_hi = (kband, kroll) if g == 0 else (kroll, kband)
                v_src_lo, v_src_hi = (vband, vroll) if g == 0 else (vroll, vband)
                k_par = (jnp.where(lo, k_src_lo, 0.0).astype(BF16), jnp.where(lo, 0.0, k_src_hi).astype(BF16))
                v_par = (jnp.where(lo, v_src_lo, 0.0).astype(BF16), jnp.where(lo, 0.0, v_src_hi).astype(BF16))
                pairs = GROUP // 2
                qp = jnp.concatenate(
                    [q[:, (g * pairs + p) * LANES:(g * pairs + p + 1) * LANES] for p in range(pairs)], axis=0)
                o_pair = None
                for par in range(2):
                    s_all = _dot_t(qp, k_par[par])
                    probs = []
                    for p in range(pairs):
                        h = g * GROUP + 2 * p + par
                        sc = s_all[p * BLOCK:(p + 1) * BLOCK, :] + bias_scr[bias_base + h]
                        probs.append(_softmax_with_sink(sc, sink_ref[l, h]).astype(BF16))
                    slot = (t * KV_HEADS + g) * 2 + par
                    if slot % slots_per_piece == 0:
                        pc = slot // slots_per_piece
                        cols = slice(pc * piece, (pc + 1) * piece)
                        y_ref[:, cols] = x_ref[:, cols] + _dot(mix_p, wo_ref[:, cols])
                    o = _dot(jnp.concatenate(probs, axis=0), v_par[par])
                    o_pair = o if o_pair is None else o_pair + o
                for p in range(pairs):
                    att_cols.append(o_pair[p * BLOCK:(p + 1) * BLOCK, :])
            att = jnp.concatenate(att_cols, axis=-1)
            mix_cur[rows, 0:ATT_WIDTH] = _rms_rows(att, goa_ref[...]).astype(BF16)

            vs = vs_ref[rows, :].astype(BF16)
            bs = bs_ref[...]
            sgu_cols = []
            for h in range(SGU_HEADS):
                sl = slice(h * SGU_HEAD_DIM, (h + 1) * SGU_HEAD_DIM)
                mixed = _dot(wtril_scr[h], vs[:, sl]) + bs[:, h:h + 1]
                sgu_cols.append(u_ref[rows, sl] * mixed)
            sgu = jnp.concatenate(sgu_cols, axis=-1)
            mix_cur[rows, ATT_WIDTH:] = _rms_rows(sgu, gos_ref[...]).astype(BF16)

    @pl.when(s % 2 == 0)
    def _even():
        step(mix0_scr, mix1_scr)

    @pl.when(s % 2 == 1)
    def _odd():
        step(mix1_scr, mix0_scr)


def _mix_prompt(l, sinks, q, k, v, u, vs, w_s, b_s_t, goa, gos, x, w_o, batch, seq):
    nb = seq // BLOCK
    n_sub = MIX_BLOCKS
    assert nb % n_sub == 0
    rows = n_sub * BLOCK
    n_steps = batch * nb // n_sub
    d = x.shape[1]
    cur_i = lambda s: jnp.minimum(s, n_steps - 1)
    lag = lambda s: jnp.maximum(s - 1, 0)
    cur = lambda width: pl.BlockSpec((rows, width), lambda s: (cur_i(s), 0))
    prev = lambda width: pl.BlockSpec(
        (BLOCK, width), lambda s: (cur_i(s) * n_sub - jnp.where((s * n_sub) % nb == 0, 0, 1), 0))
    lagged = pl.BlockSpec((rows, d), lambda s: (lag(s), 0))
    return pl.pallas_call(
        functools.partial(_mix_prompt_body, l=l, nb=nb),
        grid=(n_steps + 1,),
        in_specs=[pl.BlockSpec(memory_space=pltpu.SMEM),
                  cur(ATT_WIDTH), cur(KV_WIDTH), prev(KV_WIDTH), cur(KV_WIDTH), prev(KV_WIDTH),
                  cur(SGU_WIDTH), cur(SGU_WIDTH),
                  _layer_spec(l, (SGU_HEADS, BLOCK, BLOCK)), _layer_spec(l, (BLOCK, SGU_HEADS)),
                  _layer_spec(l, (1, ATT_WIDTH)), _layer_spec(l, (1, SGU_WIDTH)),
                  lagged, _layer_spec(l, w_o.shape[1:])],
        out_specs=lagged,
        out_shape=jax.ShapeDtypeStruct(x.shape, F32),
        scratch_shapes=[pltpu.VMEM((2 * ATT_HEADS, BLOCK, 2 * BLOCK), F32),
                        pltpu.VMEM((SGU_HEADS, BLOCK, BLOCK), BF16),
                        pltpu.VMEM((rows, ATT_WIDTH + SGU_WIDTH), BF16),
                        pltpu.VMEM((rows, ATT_WIDTH + SGU_WIDTH), BF16)],
        compiler_params=_cparams(1),
        name="mix_prompt",
    )(sinks, q, k, k, v, v, u, vs, w_s, b_s_t, goa, gos, x, w_o)


def _attn_sample_body(q_ref, kc_ref, kn_ref, vc_ref, vn_ref, sink_ref, o_ref, kw_ref, vw_ref, *, t_new):
    rows = KV_HEADS * GROUP * t_new
    keys = WINDOW + NEW_PAD
    t_shift = t_new.bit_length() - 1
    assert t_new == 1 << t_shift
    r = lax.broadcasted_iota(jnp.int32, (rows, 1), 0)
    t = jnp.bitwise_and(r, t_new - 1)
    head = lax.shift_right_logical(r, t_shift)
    j = lax.broadcasted_iota(jnp.int32, (rows, keys), 1)
    dist = WINDOW + t - j
    valid = (dist >= 0) & (dist < WINDOW)
    slope = jnp.exp2(-8.0 * (head + 1).astype(F32) / ATT_HEADS)
    bias = jnp.where(valid, -slope * dist.astype(F32), NEG_INF)
    lane_lo = lax.broadcasted_iota(jnp.int32, (rows, LANES), 1) < HEAD_DIM
    row_kv0 = lax.broadcasted_iota(jnp.int32, (rows, LANES), 0) < GROUP * t_new
    q_keep = jnp.where(lane_lo == row_kv0, 1.0, 0.0).astype(BF16)

    kx = jnp.concatenate([kc_ref[...], kn_ref[...]], axis=1)
    vx = jnp.concatenate([vc_ref[...], vn_ref[...]], axis=1)
    for w_ref, c_ref, n_ref in ((kw_ref, kc_ref, kn_ref), (vw_ref, vc_ref, vn_ref)):
        w_ref[:, 0:WINDOW - t_new, :] = c_ref[:, t_new:WINDOW, :]
        w_ref[:, WINDOW - t_new:WINDOW, :] = n_ref[:, 0:t_new, :]
    q = q_ref[...] * q_keep[None]
    s = jnp.einsum('brd,bjd->brj', q, kx.astype(BF16), preferred_element_type=F32) + bias[None]
    p = _softmax_with_sink(s, sink_ref[...][None]).astype(BF16)
    o = jnp.einsum('brj,bjd->brd', p, vx.astype(BF16), preferred_element_type=F32)
    half = GROUP * t_new
    out_lo = lax.broadcasted_iota(jnp.int32, (o.shape[0], half, LANES), 2) < HEAD_DIM
    o_ref[...] = jnp.where(out_lo, o[:, :half], o[:, half:])


def _attn_sample(l, q_ext, k_cache, k_new, v_cache, v_new, sink_rows, t_new):
    nbatch = q_ext.shape[0]
    bb = BB_ATTN
    rows = KV_HEADS * GROUP * t_new
    b3 = lambda shape: pl.BlockSpec((bb,) + shape, lambda i: (i,) + (0,) * len(shape))
    cache = pl.BlockSpec((None, bb, WINDOW, KV_WIDTH), lambda i: (l, i, 0, 0))
    return pl.pallas_call(
        functools.partial(_attn_sample_body, t_new=t_new),
        grid=(nbatch // bb,),
        in_specs=[b3((rows, LANES)), cache, b3((NEW_PAD, KV_WIDTH)), cache, b3((NEW_PAD, KV_WIDTH)),
                  _layer_spec(l, (rows, 1))],
        out_specs=[b3((rows // KV_HEADS, LANES)), b3((WINDOW, KV_WIDTH)), b3((WINDOW, KV_WIDTH))],
        out_shape=[jax.ShapeDtypeStruct((nbatch, rows // KV_HEADS, LANES), F32),
                   jax.ShapeDtypeStruct((nbatch, WINDOW, KV_WIDTH), F32),
                   jax.ShapeDtypeStruct((nbatch, WINDOW, KV_WIDTH), F32)],
        compiler_params=_cparams(1),
        name="attn_sample",
    )(q_ext, k_cache, k_new, v_cache, v_new, sink_rows)


def _merge_sample_body(att_ref, u_ref, vs_ref, wrow_ref, brow_ref, goa_ref, gos_ref, mix_ref, *, nbatch, t_new):
    mix_ref[:, 0:ATT_WIDTH] = _rms_rows(att_ref[...], goa_ref[...]).astype(mix_ref.dtype)
    for t in range(t_new):
        mixed = None
        for s in range(t + 1):
            term = wrow_ref[t * t_new + s:t * t_new + s + 1, :] * vs_ref[s * nbatch:(s + 1) * nbatch, :]
            mixed = term if mixed is None else mixed + term
        mixed = mixed + brow_ref[t:t + 1, :]
        sgu = u_ref[t * nbatch:(t + 1) * nbatch, :] * mixed
        mix_ref[t * nbatch:(t + 1) * nbatch, ATT_WIDTH:] = _rms_rows(sgu, gos_ref[...]).astype(mix_ref.dtype)


def _merge_sample(l, att, u, vs, wrow, brow, goa, gos, nbatch, t_new):
    rows = att.shape[0]
    whole = lambda a: pl.BlockSpec(a.shape, lambda i: (0,) * a.ndim)
    return pl.pallas_call(
        functools.partial(_merge_sample_body, nbatch=nbatch, t_new=t_new),
        grid=(1,),
        in_specs=[whole(att), whole(u), whole(vs), _layer_spec(l, wrow.shape[1:]), _layer_spec(l, brow.shape[1:]),
                  _layer_spec(l, (1, ATT_WIDTH)), _layer_spec(l, (1, SGU_WIDTH))],
        out_specs=pl.BlockSpec((rows, ATT_WIDTH + SGU_WIDTH), lambda i: (0, 0)),
        out_shape=jax.ShapeDtypeStruct((rows, ATT_WIDTH + SGU_WIDTH), BF16),
        compiler_params=_cparams(1),
        name="merge_sample",
    )(att, u, vs, wrow, brow, goa, gos)


def _out_proj_body(x_ref, mix_ref, w_ref, y_ref):
    y_ref[...] = x_ref[...] + _dot(mix_ref[...], w_ref[...])


def _out_proj(l, x, mix, w_o):
    rows, d = x.shape
    tm = min(TM_OUT, rows)
    return pl.pallas_call(
        _out_proj_body,
        grid=(rows // tm,),
        in_specs=[pl.BlockSpec((tm, d), lambda i: (i, 0)),
                  pl.BlockSpec((tm, mix.shape[1]), lambda i: (i, 0)),
                  _layer_spec(l, w_o.shape[1:])],
        out_specs=pl.BlockSpec((tm, d), lambda i: (i, 0)),
        out_shape=jax.ShapeDtypeStruct((rows, d), F32),
        compiler_params=_cparams(1),
        name="out_proj",
    )(x, mix, w_o)


def _silu_gate(hc_a, hc_g):
    return (jax.nn.silu(hc_g) * hc_a).astype(BF16)


def _ffn_prompt_body(x_ref, halo_ref, g_ref, wa_ref, wg_ref, cwa_ref, cwg_ref, cba_ref, cbg_ref, wd_ref,
                     y_ref, sa_ref, sg_ref, xn_scr, ha_scr, hg_scr, *, tm, tf, tiles_per_seq):
    i = pl.program_id(0)
    j = pl.program_id(1)

    @pl.when(j == 0)
    def _prologue():
        x = x_ref[...]
        xn_scr[HALO:, :] = _rms_rows(x, g_ref[...]).astype(BF16)
        keep = jnp.where((i % tiles_per_seq) == 0, 0.0, 1.0)
        xn_scr[0:HALO, :] = (_rms_rows(halo_ref[...], g_ref[...]) * keep).astype(BF16)
        y_ref[...] = x

    xe = xn_scr[...]

    def conv(h_scr, cw_ref, cb_ref, cs, r0, nr):
        hc = cb_ref[:, cs] + cw_ref[0:1, cs] * h_scr[pl.ds(HALO - 2 + r0, nr), cs]
        hc = hc + cw_ref[1:2, cs] * h_scr[pl.ds(HALO - 1 + r0, nr), cs]
        return hc + cw_ref[2:3, cs] * h_scr[pl.ds(HALO + r0, nr), cs]

    n_chunks = tf // MXU_WIDTH
    for c in range(n_chunks):
        cs = slice(c * MXU_WIDTH, (c + 1) * MXU_WIDTH)
        ha_scr[:, cs] = _dot(xe, wa_ref[:, cs])
        hg_scr[:, cs] = _dot(xe, wg_ref[:, cs])
    for c in range(n_chunks):
        cs = slice(c * MXU_WIDTH, (c + 1) * MXU_WIDTH)
        n_row = 2 if c == n_chunks - 1 else 1
        nr = tm // n_row
        for r in range(n_row):
            act = _silu_gate(conv(ha_scr, cwa_ref, cba_ref, cs, r * nr, nr),
                             conv(hg_scr, cwg_ref, cbg_ref, cs, r * nr, nr))
            y_ref[r * nr:(r + 1) * nr, :] += _dot(act, wd_ref[cs, :])
    sa_ref[...] = ha_scr[pl.ds(HALO + tm - (CONV_W - 1), CONV_W - 1), :]
    sg_ref[...] = hg_scr[pl.ds(HALO + tm - (CONV_W - 1), CONV_W - 1), :]


def _ffn_prompt(l, x, g, w_up, cw, cb, w_down, seq):
    rows, d = x.shape
    tm, tf = TM_FFN, TF_FFN
    ffp = w_down.shape[1]
    nj = ffp // tf
    tiles_per_seq = seq // tm
    halo_blocks = tm // HALO
    lyr = lambda shape, imap: pl.BlockSpec((None,) + shape, lambda i, j: (l,) + imap(i, j))
    return pl.pallas_call(
        functools.partial(_ffn_prompt_body, tm=tm, tf=tf, tiles_per_seq=tiles_per_seq),
        grid=(rows // tm, nj),
        in_specs=[pl.BlockSpec((tm, d), lambda i, j: (i, 0)),
                  pl.BlockSpec((HALO, d), lambda i, j: (jnp.maximum(i * halo_blocks - 1, 0), 0)),
                  _layer_spec(l, (1, d)),
                  lyr((d, tf), lambda i, j: (0, j)),
                  lyr((d, tf), lambda i, j: (0, j + nj)),
                  lyr((CONV_W, tf), lambda i, j: (0, j)),
                  lyr((CONV_W, tf), lambda i, j: (0, j + nj)),
                  lyr((1, tf), lambda i, j: (0, j)),
                  lyr((1, tf), lambda i, j: (0, j + nj)),
                  lyr((tf, d), lambda i, j: (j, 0))],
        out_specs=[pl.BlockSpec((tm, d), lambda i, j: (i, 0)),
                   pl.BlockSpec((None, CONV_W - 1, tf), lambda i, j: (i, 0, j)),
                   pl.BlockSpec((None, CONV_W - 1, tf), lambda i, j: (i, 0, j))],
        out_shape=[jax.ShapeDtypeStruct((rows, d), F32),
                   jax.ShapeDtypeStruct((rows // tm, CONV_W - 1, ffp), F32),
                   jax.ShapeDtypeStruct((rows // tm, CONV_W - 1, ffp), F32)],
        scratch_shapes=[pltpu.VMEM((HALO + tm, d), BF16),
                        pltpu.VMEM((HALO + tm, tf), F32),
                        pltpu.VMEM((HALO + tm, tf), F32)],
        compiler_params=_cparams(2),
        name="ffn_prompt",
    )(x, x, g, w_up, w_up, cw, cw, cb, cb, w_down)


def _ffn_sample_body(x_ref, g_ref, wa_ref, wg_ref, cwa_ref, cwg_ref, cba_ref, cbg_ref, wd_ref, bufa_ref, bufg_ref,
                     y_ref, sa_ref, sg_ref, xn_scr, *, nbatch, t_new):
    j = pl.program_id(0)

    @pl.when(j == 0)
    def _prologue():
        x = x_ref[...]
        xn_scr[...] = _rms_rows(x, g_ref[...]).astype(BF16)
        y_ref[...] = x

    xn = xn_scr[...]

    def conv(h, buf_ref, cw_ref, cb_ref, cs):
        hp = ([buf_ref[k, :, cs] for k in range(CONV_W - 1)]
              + [h[t * nbatch:(t + 1) * nbatch, :] for t in range(t_new)])
        out = []
        for t in range(t_new):
            hc = cb_ref[:, cs] + cw_ref[0:1, cs] * hp[t]
            for k in range(1, CONV_W):
                hc = hc + cw_ref[k:k + 1, cs] * hp[t + k]
            out.append(hc)
        return jnp.concatenate(out, axis=0)

    n_chunks = wa_ref.shape[1] // MXU_WIDTH
    hs = []
    for c in range(n_chunks):
        cs = slice(c * MXU_WIDTH, (c + 1) * MXU_WIDTH)
        hs.append((_dot(xn, wa_ref[:, cs]), _dot(xn, wg_ref[:, cs])))
    for c in range(n_chunks):
        cs = slice(c * MXU_WIDTH, (c + 1) * MXU_WIDTH)
        ha, hg = hs[c]
        act = _silu_gate(conv(ha, bufa_ref, cwa_ref, cba_ref, cs), conv(hg, bufg_ref, cwg_ref, cbg_ref, cs))
        y_ref[...] += _dot(act, wd_ref[cs, :])
        for k in range(CONV_W - 1):
            t = t_new - (CONV_W - 1) + k
            sa_ref[k, :, cs] = ha[t * nbatch:(t + 1) * nbatch, :]
            sg_ref[k, :, cs] = hg[t * nbatch:(t + 1) * nbatch, :]


def _ffn_sample(l, x, g, w_up, cw, cb, w_down, buf, nbatch, t_new):
    rows, d = x.shape
    tf = TF_FFN
    ffp = w_down.shape[1]
    nj = ffp // tf
    lyr = lambda shape, imap: pl.BlockSpec((None,) + shape, lambda j: (l,) + imap(j))
    state = lambda off: pl.BlockSpec((CONV_W - 1, nbatch, tf), lambda j: (0, 0, j + off))
    return pl.pallas_call(
        functools.partial(_ffn_sample_body, nbatch=nbatch, t_new=t_new),
        grid=(nj,),
        in_specs=[pl.BlockSpec((rows, d), lambda j: (0, 0)),
                  _layer_spec(l, (1, d)),
                  lyr((d, tf), lambda j: (0, j)),
                  lyr((d, tf), lambda j: (0, j + nj)),
                  lyr((CONV_W, tf), lambda j: (0, j)),
                  lyr((CONV_W, tf), lambda j: (0, j + nj)),
                  lyr((1, tf), lambda j: (0, j)),
                  lyr((1, tf), lambda j: (0, j + nj)),
                  lyr((tf, d), lambda j: (j, 0)),
                  lyr((CONV_W - 1, nbatch, tf), lambda j: (0, 0, j)),
                  lyr((CONV_W - 1, nbatch, tf), lambda j: (0, 0, j + nj))],
        out_specs=[pl.BlockSpec((rows, d), lambda j: (0, 0)), state(0), state(0)],
        out_shape=[jax.ShapeDtypeStruct((rows, d), F32),
                   jax.ShapeDtypeStruct((CONV_W - 1, nbatch, ffp), F32),
                   jax.ShapeDtypeStruct((CONV_W - 1, nbatch, ffp), F32)],
        scratch_shapes=[pltpu.VMEM((rows, d), BF16)],
        compiler_params=_cparams(1),
        name="ffn_sample",
    )(x, g, w_up, w_up, cw, cw, cb, cb, w_down, buf, buf)


def _stage_body(w_ref, o_ref, *, axis):
    n = w_ref.shape[axis]
    if axis == 0:
        o_ref[0:n, :] = w_ref[...].astype(o_ref.dtype)
        o_ref[n:, :] = jnp.zeros((o_ref.shape[0] - n, o_ref.shape[1]), o_ref.dtype)
    else:
        o_ref[:, 0:n] = w_ref[...].astype(o_ref.dtype)
        o_ref[:, n:] = jnp.zeros((o_ref.shape[0], o_ref.shape[1] - n), o_ref.dtype)


def _stage_w_up(w_up, ffp):
    depth, d, two_ff = w_up.shape
    d_ff = two_ff // 2
    tr = TR_STAGE
    return pl.pallas_call(
        functools.partial(_stage_body, axis=1),
        grid=(depth, d // tr, 2),
        in_specs=[pl.BlockSpec((None, tr, d_ff), lambda l, r, h: (l, r, h))],
        out_specs=pl.BlockSpec((None, tr, ffp), lambda l, r, h: (l, r, h)),
        out_shape=jax.ShapeDtypeStruct((depth, d, 2 * ffp), BF16),
        compiler_params=_cparams(3),
        name="stage_w_up",
    )(w_up)


def _stage_w_down(w_down, ffp):
    depth, d_ff, d = w_down.shape
    tc = TR_STAGE
    return pl.pallas_call(
        functools.partial(_stage_body, axis=0),
        grid=(depth, d // tc),
        in_specs=[pl.BlockSpec((None, d_ff, tc), lambda l, c: (l, 0, c))],
        out_specs=pl.BlockSpec((None, ffp, tc), lambda l, c: (l, 0, c)),
        out_shape=jax.ShapeDtypeStruct((depth, ffp, d), BF16),
        compiler_params=_cparams(2),
        name="stage_w_down",
    )(w_down)


def _pad_ff_halves(a, ffp, dtype):
    d_ff = a.shape[-1] // 2
    z = jnp.zeros(a.shape[:-1] + (ffp - d_ff,), dtype)
    return jnp.concatenate([a[..., :d_ff].astype(dtype), z, a[..., d_ff:].astype(dtype), z], axis=-1)


def _unpad_ff_halves(sa, sg, d_ff):
    return jnp.concatenate([sa[..., :d_ff], sg[..., :d_ff]], axis=-1)


def kernel(x_prompt, x_sample, cache_k_win, cache_v_win, state_ffn_conv, norm_mix_g, w_in, q_norm_g, k_norm_g,
           attn_sinks, sgu_ln_g, sgu_ln_b, sgu_w, sgu_b, out_norm_att_g, out_norm_sgu_g, w_o, norm_ffn_g,
           w_up, conv_w, conv_b, w_down):
    batch, seq, d = x_prompt.shape
    nbatch, t_new, _ = x_sample.shape
    depth = w_in.shape[0]
    d_ff = w_down.shape[1]
    ffp = -(-d_ff // TF_FFN) * TF_FFN

    w_in_b = w_in.astype(BF16)
    w_o_b = w_o.astype(BF16)
    w_up_b = _stage_w_up(w_up, ffp)
    w_down_b = _stage_w_down(w_down, ffp)
    cw_p = _pad_ff_halves(conv_w, ffp, F32)
    cb_p = _pad_ff_halves(conv_b, ffp, F32)[:, None, :]
    row = lambda a: a[:, None, :]
    gmix, gffn = row(norm_mix_g), row(norm_ffn_g)
    lng, lnb = row(sgu_ln_g), row(sgu_ln_b)
    goa, gos = row(out_norm_att_g), row(out_norm_sgu_g)
    gq2 = row(jnp.tile(q_norm_g, (1, 2)))
    gk2 = row(jnp.tile(k_norm_g, (1, 2)))
    b_s_t = jnp.swapaxes(sgu_b, 1, 2)
    wrow = jnp.repeat(jnp.transpose(sgu_w[:, :, :t_new, :t_new], (0, 2, 3, 1)), SGU_HEAD_DIM, axis=-1)
    wrow = wrow.reshape(depth, t_new * t_new, SGU_WIDTH)
    brow = jnp.repeat(jnp.transpose(sgu_b[:, :, :t_new], (0, 2, 1)), SGU_HEAD_DIM, axis=-1)
    sink_rows = jnp.repeat(attn_sinks, t_new, axis=-1)[..., None]
    buf_tm = _pad_ff_halves(jnp.swapaxes(state_ffn_conv, 1, 2), ffp, F32)
    k_cache = cache_k_win.reshape(depth, nbatch, WINDOW, KV_WIDTH)
    v_cache = cache_v_win.reshape(depth, nbatch, WINDOW, KV_WIDTH)

    xp = x_prompt.reshape(batch * seq, d)
    xs = jnp.swapaxes(x_sample, 0, 1).reshape(t_new * nbatch, d)
    kp_l, vp_l, cp_l, ks_l, vs_l, cs_l, sv_l = [], [], [], [], [], [], []
    for l in range(depth):
        q, k, v, u, vs = _proj(l, xp, gmix, w_in_b, gq2, gk2, lng, lnb)
        xp = _mix_prompt(l, attn_sinks, q, k, v, u, vs, sgu_w, b_s_t, goa, gos, xp, w_o_b, batch, seq)
        xp, sa, sg = _ffn_prompt(l, xp, gffn, w_up_b, cw_p, cb_p, w_down_b, seq)
        kp_l.append(k.reshape(batch, seq, KV_HEADS, HEAD_DIM)[:, -WINDOW:])
        vp_l.append(v.reshape(batch, seq, KV_HEADS, HEAD_DIM)[:, -WINDOW:])
        tiles_per_seq = seq // TM_FFN
        cp_l.append(_unpad_ff_halves(sa, sg, d_ff)[tiles_per_seq - 1:batch * tiles_per_seq:tiles_per_seq])

        q, k, v, u, vs = _proj(l, xs, gmix, w_in_b, gq2, gk2, lng, lnb)
        pad_new = ((0, 0), (0, NEW_PAD - t_new), (0, 0))
        k_new = jnp.pad(jnp.swapaxes(k.reshape(t_new, nbatch, KV_WIDTH), 0, 1), pad_new)
        v_new = jnp.pad(jnp.swapaxes(v.reshape(t_new, nbatch, KV_WIDTH), 0, 1), pad_new)
        q5 = jnp.transpose(q.reshape(t_new, nbatch, ATT_HEADS, HEAD_DIM), (1, 2, 0, 3))
        q5 = q5.reshape(nbatch, ATT_HEADS * t_new, HEAD_DIM)
        q_ext = jnp.concatenate([q5, q5], axis=-1)
        o_ext, k_win, v_win = _attn_sample(l, q_ext, k_cache, k_new, v_cache, v_new, sink_rows, t_new)
        o5 = o_ext.reshape(nbatch, GROUP, t_new, KV_HEADS, HEAD_DIM)
        att = jnp.transpose(o5, (2, 0, 3, 1, 4)).reshape(t_new * nbatch, ATT_WIDTH)
        mix = _merge_sample(l, att, u, vs, wrow, brow, goa, gos, nbatch, t_new)
        xs = _out_proj(l, xs, mix, w_o_b)
        xs, sa, sg = _ffn_sample(l, xs, gffn, w_up_b, cw_p, cb_p, w_down_b, buf_tm, nbatch, t_new)
        ks_l.append(k_win.reshape(nbatch, WINDOW, KV_HEADS, HEAD_DIM))
        vs_l.append(v_win.reshape(nbatch, WINDOW, KV_HEADS, HEAD_DIM))
        cs_l.append(jnp.swapaxes(_unpad_ff_halves(sa, sg, d_ff), 0, 1))
        sv_l.append(jnp.swapaxes(vs.reshape(t_new, nbatch, SGU_HEADS, SGU_HEAD_DIM), 0, 1))

    y_prompt = xp.reshape(batch, seq, d)
    y_sample = jnp.swapaxes(xs.reshape(t_new, nbatch, d), 0, 1)
    return (y_prompt, y_sample, jnp.stack(kp_l), jnp.stack(vp_l), jnp.stack(cp_l),
            jnp.stack(ks_l), jnp.stack(vs_l), jnp.stack(cs_l), jnp.stack(sv_l))
```

```python
import functools

import jax
import jax.numpy as jnp
from jax import lax
from jax.experimental import pallas as pl
from jax.experimental.pallas import tpu as pltpu

F32 = jnp.float32
BF16 = jnp.bfloat16

HEAD_DIM = 64
ATT_HEADS = 16
KV_HEADS = 2
GROUP = ATT_HEADS // KV_HEADS
ATT_WIDTH = ATT_HEADS * HEAD_DIM
KV_WIDTH = KV_HEADS * HEAD_DIM
SGU_HEADS = 8
SGU_HEAD_DIM = 128
SGU_WIDTH = SGU_HEADS * SGU_HEAD_DIM
WINDOW = 128
BLOCK = 128
CONV_W = 3
EPS = 1e-6
NEG_INF = -1e30
LANES = 128
MXU_WIDTH = 256
HALO = 16
NEW_PAD = 16
VMEM_LIMIT = 60 * 1024 * 1024

TM_PROJ = 256
MIX_BLOCKS = 2
TM_OUT = 512
TM_FFN = 1024
TF_FFN = 512
BB_ATTN = 32
TR_STAGE = 512

C_Q = ATT_WIDTH
C_K = C_Q + KV_WIDTH
C_V = C_K + KV_WIDTH
C_U = C_V + SGU_WIDTH


def _cparams(n_axes, flags=None):
    return pltpu.CompilerParams(dimension_semantics=("arbitrary",) * n_axes,
                                vmem_limit_bytes=VMEM_LIMIT, flags=flags)


def _layer_spec(l, shape):
    zeros = (0,) * len(shape)
    return pl.BlockSpec((None,) + tuple(shape), lambda *_: (l,) + zeros)


def _rms_rows(x, g):
    y = x * lax.rsqrt(jnp.mean(x * x, axis=-1, keepdims=True) + EPS)
    return y * g


def _rms_head_pairs(z, g2):
    lo = lax.broadcasted_iota(jnp.int32, z.shape, 1) < HEAD_DIM
    z2 = z * z
    s_lo = jnp.sum(jnp.where(lo, z2, 0.0), axis=-1, keepdims=True)
    s_hi = jnp.sum(jnp.where(lo, 0.0, z2), axis=-1, keepdims=True)
    inv = jnp.where(lo, lax.rsqrt(s_lo / HEAD_DIM + EPS), lax.rsqrt(s_hi / HEAD_DIM + EPS))
    return (z * inv) * g2


def _dot(a, b):
    return jnp.dot(a, b, preferred_element_type=F32)


def _dot_t(a, b):
    return lax.dot_general(a, b, (((1,), (1,)), ((), ())), preferred_element_type=F32)


def _proj_body(x_ref, gmix_ref, w_ref, gq_ref, gk_ref, lng_ref, lnb_ref,
               q_ref, k_ref, v_ref, u_ref, vs_ref):
    xn = _rms_rows(x_ref[...], gmix_ref[...]).astype(BF16)
    gv = jax.nn.gelu(_dot(xn, w_ref[:, C_U:]))
    mu = jnp.mean(gv, axis=-1, keepdims=True)
    xc = gv - mu
    y = xc * lax.rsqrt(jnp.mean(xc * xc, axis=-1, keepdims=True) + EPS)
    vs_ref[...] = y * lng_ref[...] + lnb_ref[...]
    zq = _dot(xn, w_ref[:, 0:C_Q])
    gq = gq_ref[...]
    for c in range(ATT_WIDTH // LANES):
        sl = slice(c * LANES, (c + 1) * LANES)
        q_ref[:, sl] = (_rms_head_pairs(zq[:, sl], gq) * (HEAD_DIM ** -0.5)).astype(q_ref.dtype)
    zkv = _dot(xn, w_ref[:, C_Q:C_V])
    k_ref[...] = _rms_head_pairs(zkv[:, 0:KV_WIDTH], gk_ref[...])
    v_ref[...] = zkv[:, KV_WIDTH:]
    u_ref[...] = jax.nn.gelu(_dot(xn, w_ref[:, C_V:C_U]))


def _proj(l, x, gmix, w_in, gq2, gk2, lng, lnb):
    rows, d = x.shape
    tm = min(TM_PROJ, rows)
    in_cols = w_in.shape[-1]
    row_spec = lambda width: pl.BlockSpec((tm, width), lambda i: (i, 0))
    return pl.pallas_call(
        _proj_body,
        grid=(rows // tm,),
        in_specs=[row_spec(d), _layer_spec(l, (1, d)), _layer_spec(l, (d, in_cols)),
                  _layer_spec(l, (1, LANES)), _layer_spec(l, (1, LANES)),
                  _layer_spec(l, (1, SGU_WIDTH)), _layer_spec(l, (1, SGU_WIDTH))],
        out_specs=[row_spec(ATT_WIDTH), row_spec(KV_WIDTH), row_spec(KV_WIDTH),
                   row_spec(SGU_WIDTH), row_spec(SGU_WIDTH)],
        out_shape=[jax.ShapeDtypeStruct((rows, ATT_WIDTH), BF16),
                   jax.ShapeDtypeStruct((rows, KV_WIDTH), F32),
                   jax.ShapeDtypeStruct((rows, KV_WIDTH), F32),
                   jax.ShapeDtypeStruct((rows, SGU_WIDTH), F32),
                   jax.ShapeDtypeStruct((rows, SGU_WIDTH), F32)],
        compiler_params=_cparams(1),
        name="proj",
    )(x, gmix, w_in, gq2, gk2, lng, lnb)


def _alibi_slope(h):
    return 2.0 ** (-8.0 * (h + 1) / ATT_HEADS)


def _softmax_with_sink(s, sink):
    m = jnp.maximum(jnp.max(s, axis=-1, keepdims=True), sink)
    e = jnp.exp(s - m)
    denom = jnp.sum(e, axis=-1, keepdims=True) + jnp.exp(sink - m)
    return e / denom


def _mix_prompt_body(sink_ref, q_ref, kc_ref, kp_ref, vc_ref, vp_ref, u_ref, vs_ref, ws_ref, bs_ref,
                     goa_ref, gos_ref, x_ref, wo_ref, y_ref, bias_scr, wtril_scr, mix0_scr, mix1_scr, *, l, nb):
    s = pl.program_id(0)

    @pl.when(s == 0)
    def _init():
        mix1_scr[...] = jnp.zeros_like(mix1_scr)
        row = lax.broadcasted_iota(jnp.int32, (BLOCK, 2 * BLOCK), 0)
        col = lax.broadcasted_iota(jnp.int32, (BLOCK, 2 * BLOCK), 1)
        dist = BLOCK + row - col
        valid = (dist >= 0) & (dist < WINDOW)
        distf = dist.astype(F32)
        for h in range(ATT_HEADS):
            bias = jnp.where(valid, -_alibi_slope(h) * distf, NEG_INF)
            bias_scr[h] = bias
            bias_scr[ATT_HEADS + h] = jnp.where(col < BLOCK, NEG_INF, bias)
        r = lax.broadcasted_iota(jnp.int32, (BLOCK, BLOCK), 0)
        c = lax.broadcasted_iota(jnp.int32, (BLOCK, BLOCK), 1)
        for h in range(SGU_HEADS):
            wtril_scr[h] = jnp.where(r >= c, ws_ref[h], 0.0).astype(BF16)

    d_out = y_ref.shape[1]
    n_sub = q_ref.shape[0] // BLOCK
    n_slots = n_sub * KV_HEADS * 2
    n_pieces = min(n_slots, d_out // MXU_WIDTH)
    slots_per_piece = n_slots // n_pieces
    piece = d_out // n_pieces
    first_of_seq = (s * n_sub) % nb == 0

    def step(mix_cur, mix_prev):
        lo = lax.broadcasted_iota(jnp.int32, (2 * BLOCK, LANES), 1) < HEAD_DIM
        mix_p = mix_prev[...]
        for t in range(n_sub):
            rows = slice(t * BLOCK, (t + 1) * BLOCK)
            if t == 0:
                bias_base = jnp.where(first_of_seq, ATT_HEADS, 0)
                k_prev, v_prev = kp_ref[...], vp_ref[...]
            else:
                bias_base = 0
                k_prev, v_prev = kc_ref[(t - 1) * BLOCK:t * BLOCK, :], vc_ref[(t - 1) * BLOCK:t * BLOCK, :]
            kband = jnp.concatenate([k_prev, kc_ref[rows, :]], axis=0)
            vband = jnp.concatenate([v_prev, vc_ref[rows, :]], axis=0)
            kroll = pltpu.roll(kband, HEAD_DIM, 1)
            vroll = pltpu.roll(vband, HEAD_DIM, 1)
            q = q_ref[rows, :]
            att_cols = []
            for g in range(KV_HEADS):
                k_src_lo, k_src_hi = (kband, kroll) if g == 0 else (kroll, kband)
                v_src_lo, v_src_hi = (vband, vroll) if g == 0 else (vroll, vband)
                k_par = (jnp.where(lo, k_src_lo, 0.0).astype(BF16), jnp.where(lo, 0.0, k_src_hi).astype(BF16))
                v_par = (jnp.where(lo, v_src_lo, 0.0).astype(BF16), jnp.where(lo, 0.0, v_src_hi).astype(BF16))
                pairs = GROUP // 2
                qp = jnp.concatenate(
                    [q[:, (g * pairs + p) * LANES:(g * pairs + p + 1) * LANES] for p in range(pairs)], axis=0)
                o_pair = None
                for par in range(2):
                    s_all = _dot_t(qp, k_par[par])
                    probs = []
                    for p in range(pairs):
                        h = g * GROUP + 2 * p + par
                        sc = s_all[p * BLOCK:(p + 1) * BLOCK, :] + bias_scr[bias_base + h]
                        probs.append(_softmax_with_sink(sc, sink_ref[l, h]).astype(BF16))
                    slot = (t * KV_HEADS + g) * 2 + par
                    if slot % slots_per_piece == 0:
                        pc = slot // slots_per_piece
                        cols = slice(pc * piece, (pc + 1) * piece)
                        y_ref[:, cols] = x_ref[:, cols] + _dot(mix_p, wo_ref[:, cols])
                    o = _dot(jnp.concatenate(probs, axis=0), v_par[par])
                    o_pair = o if o_pair is None else o_pair + o
                for p in range(pairs):
                    att_cols.append(o_pair[p * BLOCK:(p + 1) * BLOCK, :])
            att = jnp.concatenate(att_cols, axis=-1)
            mix_cur[rows, 0:ATT_WIDTH] = _rms_rows(att, goa_ref[...]).astype(BF16)

            vs = vs_ref[rows, :].astype(BF16)
            bs = bs_ref[...]
            sgu_cols = []
            for h in range(SGU_HEADS):
                sl = slice(h * SGU_HEAD_DIM, (h + 1) * SGU_HEAD_DIM)
                mixed = _dot(wtril_scr[h], vs[:, sl]) + bs[:, h:h + 1]
                sgu_cols.append(u_ref[rows, sl] * mixed)
            sgu = jnp.concatenate(sgu_cols, axis=-1)
            mix_cur[rows, ATT_WIDTH:] = _rms_rows(sgu, gos_ref[...]).astype(BF16)

    @pl.when(s % 2 == 0)
    def _even():
        step(mix0_scr, mix1_scr)

    @pl.when(s % 2 == 1)
    def _odd():
        step(mix1_scr, mix0_scr)


def _mix_prompt(l, sinks, q, k, v, u, vs, w_s, b_s_t, goa, gos, x, w_o, batch, seq):
    nb = seq // BLOCK
    n_sub = MIX_BLOCKS
    assert nb % n_sub == 0
    rows = n_sub * BLOCK
    n_steps = batch * nb // n_sub
    d = x.shape[1]
    cur_i = lambda s: jnp.minimum(s, n_steps - 1)
    lag = lambda s: jnp.maximum(s - 1, 0)
    cur = lambda width: pl.BlockSpec((rows, width), lambda s: (cur_i(s), 0))
    prev = lambda width: pl.BlockSpec(
        (BLOCK, width), lambda s: (cur_i(s) * n_sub - jnp.where((s * n_sub) % nb == 0, 0, 1), 0))
    lagged = pl.BlockSpec((rows, d), lambda s: (lag(s), 0))
    return pl.pallas_call(
        functools.partial(_mix_prompt_body, l=l, nb=nb),
        grid=(n_steps + 1,),
        in_specs=[pl.BlockSpec(memory_space=pltpu.SMEM),
                  cur(ATT_WIDTH), cur(KV_WIDTH), prev(KV_WIDTH), cur(KV_WIDTH), prev(KV_WIDTH),
                  cur(SGU_WIDTH), cur(SGU_WIDTH),
                  _layer_spec(l, (SGU_HEADS, BLOCK, BLOCK)), _layer_spec(l, (BLOCK, SGU_HEADS)),
                  _layer_spec(l, (1, ATT_WIDTH)), _layer_spec(l, (1, SGU_WIDTH)),
                  lagged, _layer_spec(l, w_o.shape[1:])],
        out_specs=lagged,
        out_shape=jax.ShapeDtypeStruct(x.shape, F32),
        scratch_shapes=[pltpu.VMEM((2 * ATT_HEADS, BLOCK, 2 * BLOCK), F32),
                        pltpu.VMEM((SGU_HEADS, BLOCK, BLOCK), BF16),
                        pltpu.VMEM((rows, ATT_WIDTH + SGU_WIDTH), BF16),
                        pltpu.VMEM((rows, ATT_WIDTH + SGU_WIDTH), BF16)],
        compiler_params=_cparams(1),
        name="mix_prompt",
    )(sinks, q, k, k, v, v, u, vs, w_s, b_s_t, goa, gos, x, w_o)


def _attn_sample_body(q_ref, kc_ref, kn_ref, vc_ref, vn_ref, sink_ref, o_ref, kw_ref, vw_ref, *, t_new):
    rows = KV_HEADS * GROUP * t_new
    keys = WINDOW + NEW_PAD
    t_shift = t_new.bit_length() - 1
    assert t_new == 1 << t_shift
    r = lax.broadcasted_iota(jnp.int32, (rows, 1), 0)
    t = jnp.bitwise_and(r, t_new - 1)
    head = lax.shift_right_logical(r, t_shift)
    j = lax.broadcasted_iota(jnp.int32, (rows, keys), 1)
    dist = WINDOW + t - j
    valid = (dist >= 0) & (dist < WINDOW)
    slope = jnp.exp2(-8.0 * (head + 1).astype(F32) / ATT_HEADS)
    bias = jnp.where(valid, -slope * dist.astype(F32), NEG_INF)
    lane_lo = lax.broadcasted_iota(jnp.int32, (rows, LANES), 1) < HEAD_DIM
    row_kv0 = lax.broadcasted_iota(jnp.int32, (rows, LANES), 0) < GROUP * t_new
    q_keep = jnp.where(lane_lo == row_kv0, 1.0, 0.0).astype(BF16)

    kx = jnp.concatenate([kc_ref[...], kn_ref[...]], axis=1)
    vx = jnp.concatenate([vc_ref[...], vn_ref[...]], axis=1)
    for w_ref, c_ref, n_ref in ((kw_ref, kc_ref, kn_ref), (vw_ref, vc_ref, vn_ref)):
        w_ref[:, 0:WINDOW - t_new, :] = c_ref[:, t_new:WINDOW, :]
        w_ref[:, WINDOW - t_new:WINDOW, :] = n_ref[:, 0:t_new, :]
    q = q_ref[...] * q_keep[None]
    s = jnp.einsum('brd,bjd->brj', q, kx.astype(BF16), preferred_element_type=F32) + bias[None]
    p = _softmax_with_sink(s, sink_ref[...][None]).astype(BF16)
    o = jnp.einsum('brj,bjd->brd', p, vx.astype(BF16), preferred_element_type=F32)
    half = GROUP * t_new
    out_lo = lax.broadcasted_iota(jnp.int32, (o.shape[0], half, LANES), 2) < HEAD_DIM
    o_ref[...] = jnp.where(out_lo, o[:, :half], o[:, half:])


def _attn_sample(l, q_ext, k_cache, k_new, v_cache, v_new, sink_rows, t_new):
    nbatch = q_ext.shape[0]
    bb = BB_ATTN
    rows = KV_HEADS * GROUP * t_new
    b3 = lambda shape: pl.BlockSpec((bb,) + shape, lambda i: (i,) + (0,) * len(shape))
    cache = pl.BlockSpec((None, bb, WINDOW, KV_WIDTH), lambda i: (l, i, 0, 0))
    return pl.pallas_call(
        functools.partial(_attn_sample_body, t_new=t_new),
        grid=(nbatch // bb,),
        in_specs=[b3((rows, LANES)), cache, b3((NEW_PAD, KV_WIDTH)), cache, b3((NEW_PAD, KV_WIDTH)),
                  _layer_spec(l, (rows, 1))],
        out_specs=[b3((rows // KV_HEADS, LANES)), b3((WINDOW, KV_WIDTH)), b3((WINDOW, KV_WIDTH))],
        out_shape=[jax.ShapeDtypeStruct((nbatch, rows // KV_HEADS, LANES), F32),
                   jax.ShapeDtypeStruct((nbatch, WINDOW, KV_WIDTH), F32),
                   jax.ShapeDtypeStruct((nbatch, WINDOW, KV_WIDTH), F32)],
        compiler_params=_cparams(1),
        name="attn_sample",
    )(q_ext, k_cache, k_new, v_cache, v_new, sink_rows)


def _merge_sample_body(att_ref, u_ref, vs_ref, wrow_ref, brow_ref, goa_ref, gos_ref, mix_ref, *, nbatch, t_new):
    mix_ref[:, 0:ATT_WIDTH] = _rms_rows(att_ref[...], goa_ref[...]).astype(mix_ref.dtype)
    for t in range(t_new):
        mixed = None
        for s in range(t + 1):
            term = wrow_ref[t * t_new + s:t * t_new + s + 1, :] * vs_ref[s * nbatch:(s + 1) * nbatch, :]
            mixed = term if mixed is None else mixed + term
        mixed = mixed + brow_ref[t:t + 1, :]
        sgu = u_ref[t * nbatch:(t + 1) * nbatch, :] * mixed
        mix_ref[t * nbatch:(t + 1) * nbatch, ATT_WIDTH:] = _rms_rows(sgu, gos_ref[...]).astype(mix_ref.dtype)


def _merge_sample(l, att, u, vs, wrow, brow, goa, gos, nbatch, t_new):
    rows = att.shape[0]
    whole = lambda a: pl.BlockSpec(a.shape, lambda i: (0,) * a.ndim)
    return pl.pallas_call(
        functools.partial(_merge_sample_body, nbatch=nbatch, t_new=t_new),
        grid=(1,),
        in_specs=[whole(att), whole(u), whole(vs), _layer_spec(l, wrow.shape[1:]), _layer_spec(l, brow.shape[1:]),
                  _layer_spec(l, (1, ATT_WIDTH)), _layer_spec(l, (1, SGU_WIDTH))],
        out_specs=pl.BlockSpec((rows, ATT_WIDTH + SGU_WIDTH), lambda i: (0, 0)),
        out_shape=jax.ShapeDtypeStruct((rows, ATT_WIDTH + SGU_WIDTH), BF16),
        compiler_params=_cparams(1),
        name="merge_sample",
    )(att, u, vs, wrow, brow, goa, gos)


def _out_proj_body(x_ref, mix_ref, w_ref, y_ref):
    y_ref[...] = x_ref[...] + _dot(mix_ref[...], w_ref[...])


def _out_proj(l, x, mix, w_o):
    rows, d = x.shape
    tm = min(TM_OUT, rows)
    return pl.pallas_call(
        _out_proj_body,
        grid=(rows // tm,),
        in_specs=[pl.BlockSpec((tm, d), lambda i: (i, 0)),
                  pl.BlockSpec((tm, mix.shape[1]), lambda i: (i, 0)),
                  _layer_spec(l, w_o.shape[1:])],
        out_specs=pl.BlockSpec((tm, d), lambda i: (i, 0)),
        out_shape=jax.ShapeDtypeStruct((rows, d), F32),
        compiler_params=_cparams(1),
        name="out_proj",
    )(x, mix, w_o)


def _silu_gate(hc_a, hc_g):
    return (jax.nn.silu(hc_g) * hc_a).astype(BF16)


def _ffn_prompt_body(x_ref, halo_ref, g_ref, wa_ref, wg_ref, cwa_ref, cwg_ref, cba_ref, cbg_ref, wd_ref,
                     y_ref, sa_ref, sg_ref, xn_scr, ha_scr, hg_scr, *, tm, tf, tiles_per_seq):
    i = pl.program_id(0)
    j = pl.program_id(1)

    @pl.when(j == 0)
    def _prologue():
        x = x_ref[...]
        xn_scr[HALO:, :] = _rms_rows(x, g_ref[...]).astype(BF16)
        keep = jnp.where((i % tiles_per_seq) == 0, 0.0, 1.0)
        xn_scr[0:HALO, :] = (_rms_rows(halo_ref[...], g_ref[...]) * keep).astype(BF16)
        y_ref[...] = x

    xe = xn_scr[...]

    def conv(h_scr, cw_ref, cb_ref, cs, r0, nr):
        hc = cb_ref[:, cs] + cw_ref[0:1, cs] * h_scr[pl.ds(HALO - 2 + r0, nr), cs]
        hc = hc + cw_ref[1:2, cs] * h_scr[pl.ds(HALO - 1 + r0, nr), cs]
        return hc + cw_ref[2:3, cs] * h_scr[pl.ds(HALO + r0, nr), cs]

    n_chunks = tf // MXU_WIDTH
    for c in range(n_chunks):
        cs = slice(c * MXU_WIDTH, (c + 1) * MXU_WIDTH)
        ha_scr[:, cs] = _dot(xe, wa_ref[:, cs])
        hg_scr[:, cs] = _dot(xe, wg_ref[:, cs])
    for c in range(n_chunks):
        cs = slice(c * MXU_WIDTH, (c + 1) * MXU_WIDTH)
        first = tm * 5 // 8 if c == n_chunks - 1 else tm
        for r0, nr in ((0, first), (first, tm - first)):
            if nr == 0:
                continue
            act = _silu_gate(conv(ha_scr, cwa_ref, cba_ref, cs, r0, nr),
                             conv(hg_scr, cwg_ref, cbg_ref, cs, r0, nr))
            y_ref[r0:r0 + nr, :] += _dot(act, wd_ref[cs, :])
    sa_ref[...] = ha_scr[pl.ds(HALO + tm - (CONV_W - 1), CONV_W - 1), :]
    sg_ref[...] = hg_scr[pl.ds(HALO + tm - (CONV_W - 1), CONV_W - 1), :]


def _ffn_prompt(l, x, g, w_up, cw, cb, w_down, seq):
    rows, d = x.shape
    tm, tf = TM_FFN, TF_FFN
    ffp = w_down.shape[1]
    nj = ffp // tf
    tiles_per_seq = seq // tm
    halo_blocks = tm // HALO
    lyr = lambda shape, imap: pl.BlockSpec((None,) + shape, lambda i, j: (l,) + imap(i, j))
    return pl.pallas_call(
        functools.partial(_ffn_prompt_body, tm=tm, tf=tf, tiles_per_seq=tiles_per_seq),
        grid=(rows // tm, nj),
        in_specs=[pl.BlockSpec((tm, d), lambda i, j: (i, 0)),
                  pl.BlockSpec((HALO, d), lambda i, j: (jnp.maximum(i * halo_blocks - 1, 0), 0)),
                  _layer_spec(l, (1, d)),
                  lyr((d, tf), lambda i, j: (0, j)),
                  lyr((d, tf), lambda i, j: (0, j + nj)),
                  lyr((CONV_W, tf), lambda i, j: (0, j)),
                  lyr((CONV_W, tf), lambda i, j: (0, j + nj)),
                  lyr((1, tf), lambda i, j: (0, j)),
                  lyr((1, tf), lambda i, j: (0, j + nj)),
                  lyr((tf, d), lambda i, j: (j, 0))],
        out_specs=[pl.BlockSpec((tm, d), lambda i, j: (i, 0)),
                   pl.BlockSpec((None, CONV_W - 1, tf), lambda i, j: (i, 0, j)),
                   pl.BlockSpec((None, CONV_W - 1, tf), lambda i, j: (i, 0, j))],
        out_shape=[jax.ShapeDtypeStruct((rows, d), F32),
                   jax.ShapeDtypeStruct((rows // tm, CONV_W - 1, ffp), F32),
                   jax.ShapeDtypeStruct((rows // tm, CONV_W - 1, ffp), F32)],
        scratch_shapes=[pltpu.VMEM((HALO + tm, d), BF16),
                        pltpu.VMEM((HALO + tm, tf), F32),
                        pltpu.VMEM((HALO + tm, tf), F32)],
        compiler_params=_cparams(2),
        name="ffn_prompt",
    )(x, x, g, w_up, w_up, cw, cw, cb, cb, w_down)


def _ffn_sample_body(x_ref, g_ref, wa_ref, wg_ref, cwa_ref, cwg_ref, cba_ref, cbg_ref, wd_ref, bufa_ref, bufg_ref,
                     y_ref, sa_ref, sg_ref, xn_scr, *, nbatch, t_new):
    j = pl.program_id(0)

    @pl.when(j == 0)
    def _prologue():
        x = x_ref[...]
        xn_scr[...] = _rms_rows(x, g_ref[...]).astype(BF16)
        y_ref[...] = x

    xn = xn_scr[...]

    def conv(h, buf_ref, cw_ref, cb_ref, cs):
        hp = ([buf_ref[k, :, cs] for k in range(CONV_W - 1)]
              + [h[t * nbatch:(t + 1) * nbatch, :] for t in range(t_new)])
        out = []
        for t in range(t_new):
            hc = cb_ref[:, cs] + cw_ref[0:1, cs] * hp[t]
            for k in range(1, CONV_W):
                hc = hc + cw_ref[k:k + 1, cs] * hp[t + k]
            out.append(hc)
        return jnp.concatenate(out, axis=0)

    n_chunks = wa_ref.shape[1] // MXU_WIDTH
    hs = []
    for c in range(n_chunks):
        cs = slice(c * MXU_WIDTH, (c + 1) * MXU_WIDTH)
        hs.append((_dot(xn, wa_ref[:, cs]), _dot(xn, wg_ref[:, cs])))
    for c in range(n_chunks):
        cs = slice(c * MXU_WIDTH, (c + 1) * MXU_WIDTH)
        ha, hg = hs[c]
        act = _silu_gate(conv(ha, bufa_ref, cwa_ref, cba_ref, cs), conv(hg, bufg_ref, cwg_ref, cbg_ref, cs))
        y_ref[...] += _dot(act, wd_ref[cs, :])
        for k in range(CONV_W - 1):
            t = t_new - (CONV_W - 1) + k
            sa_ref[k, :, cs] = ha[t * nbatch:(t + 1) * nbatch, :]
            sg_ref[k, :, cs] = hg[t * nbatch:(t + 1) * nbatch, :]


def _ffn_sample(l, x, g, w_up, cw, cb, w_down, buf, nbatch, t_new):
    rows, d = x.shape
    tf = TF_FFN
    ffp = w_down.shape[1]
    nj = ffp // tf
    lyr = lambda shape, imap: pl.BlockSpec((None,) + shape, lambda j: (l,) + imap(j))
    state = lambda off: pl.BlockSpec((CONV_W - 1, nbatch, tf), lambda j: (0, 0, j + off))
    return pl.pallas_call(
        functools.partial(_ffn_sample_body, nbatch=nbatch, t_new=t_new),
        grid=(nj,),
        in_specs=[pl.BlockSpec((rows, d), lambda j: (0, 0)),
                  _layer_spec(l, (1, d)),
                  lyr((d, tf), lambda j: (0, j)),
                  lyr((d, tf), lambda j: (0, j + nj)),
                  lyr((CONV_W, tf), lambda j: (0, j)),
                  lyr((CONV_W, tf), lambda j: (0, j + nj)),
                  lyr((1, tf), lambda j: (0, j)),
                  lyr((1, tf), lambda j: (0, j + nj)),
                  lyr((tf, d), lambda j: (j, 0)),
                  lyr((CONV_W - 1, nbatch, tf), lambda j: (0, 0, j)),
                  lyr((CONV_W - 1, nbatch, tf), lambda j: (0, 0, j + nj))],
        out_specs=[pl.BlockSpec((rows, d), lambda j: (0, 0)), state(0), state(0)],
        out_shape=[jax.ShapeDtypeStruct((rows, d), F32),
                   jax.ShapeDtypeStruct((CONV_W - 1, nbatch, ffp), F32),
                   jax.ShapeDtypeStruct((CONV_W - 1, nbatch, ffp), F32)],
        scratch_shapes=[pltpu.VMEM((rows, d), BF16)],
        compiler_params=_cparams(1),
        name="ffn_sample",
    )(x, g, w_up, w_up, cw, cw, cb, cb, w_down, buf, buf)


def _stage_body(w_ref, o_ref, *, axis):
    n = w_ref.shape[axis]
    if axis == 0:
        o_ref[0:n, :] = w_ref[...].astype(o_ref.dtype)
        o_ref[n:, :] = jnp.zeros((o_ref.shape[0] - n, o_ref.shape[1]), o_ref.dtype)
    else:
        o_ref[:, 0:n] = w_ref[...].astype(o_ref.dtype)
        o_ref[:, n:] = jnp.zeros((o_ref.shape[0], o_ref.shape[1] - n), o_ref.dtype)


def _stage_w_up(w_up, ffp):
    depth, d, two_ff = w_up.shape
    d_ff = two_ff // 2
    tr = TR_STAGE
    return pl.pallas_call(
        functools.partial(_stage_body, axis=1),
        grid=(depth, d // tr, 2),
        in_specs=[pl.BlockSpec((None, tr, d_ff), lambda l, r, h: (l, r, h))],
        out_specs=pl.BlockSpec((None, tr, ffp), lambda l, r, h: (l, r, h)),
        out_shape=jax.ShapeDtypeStruct((depth, d, 2 * ffp), BF16),
        compiler_params=_cparams(3),
        name="stage_w_up",
    )(w_up)


def _stage_w_down(w_down, ffp):
    depth, d_ff, d = w_down.shape
    tc = TR_STAGE
    return pl.pallas_call(
        functools.partial(_stage_body, axis=0),
        grid=(depth, d // tc),
        in_specs=[pl.BlockSpec((None, d_ff, tc), lambda l, c: (l, 0, c))],
        out_specs=pl.BlockSpec((None, ffp, tc), lambda l, c: (l, 0, c)),
        out_shape=jax.ShapeDtypeStruct((depth, ffp, d), BF16),
        compiler_params=_cparams(2),
        name="stage_w_down",
    )(w_down)


def _pad_ff_halves(a, ffp, dtype):
    d_ff = a.shape[-1] // 2
    z = jnp.zeros(a.shape[:-1] + (ffp - d_ff,), dtype)
    return jnp.concatenate([a[..., :d_ff].astype(dtype), z, a[..., d_ff:].astype(dtype), z], axis=-1)


def _unpad_ff_halves(sa, sg, d_ff):
    return jnp.concatenate([sa[..., :d_ff], sg[..., :d_ff]], axis=-1)


def kernel(x_prompt, x_sample, cache_k_win, cache_v_win, state_ffn_conv, norm_mix_g, w_in, q_norm_g, k_norm_g,
           attn_sinks, sgu_ln_g, sgu_ln_b, sgu_w, sgu_b, out_norm_att_g, out_norm_sgu_g, w_o, norm_ffn_g,
           w_up, conv_w, conv_b, w_down):
    batch, seq, d = x_prompt.shape
    nbatch, t_new, _ = x_sample.shape
    depth = w_in.shape[0]
    d_ff = w_down.shape[1]
    ffp = -(-d_ff // TF_FFN) * TF_FFN

    w_in_b = w_in.astype(BF16)
    w_o_b = w_o.astype(BF16)
    w_up_b = _stage_w_up(w_up, ffp)
    w_down_b = _stage_w_down(w_down, ffp)
    cw_p = _pad_ff_halves(conv_w, ffp, F32)
    cb_p = _pad_ff_halves(conv_b, ffp, F32)[:, None, :]
    row = lambda a: a[:, None, :]
    gmix, gffn = row(norm_mix_g), row(norm_ffn_g)
    lng, lnb = row(sgu_ln_g), row(sgu_ln_b)
    goa, gos = row(out_norm_att_g), row(out_norm_sgu_g)
    gq2 = row(jnp.tile(q_norm_g, (1, 2)))
    gk2 = row(jnp.tile(k_norm_g, (1, 2)))
    b_s_t = jnp.swapaxes(sgu_b, 1, 2)
    wrow = jnp.repeat(jnp.transpose(sgu_w[:, :, :t_new, :t_new], (0, 2, 3, 1)), SGU_HEAD_DIM, axis=-1)
    wrow = wrow.reshape(depth, t_new * t_new, SGU_WIDTH)
    brow = jnp.repeat(jnp.transpose(sgu_b[:, :, :t_new], (0, 2, 1)), SGU_HEAD_DIM, axis=-1)
    sink_rows = jnp.repeat(attn_sinks, t_new, axis=-1)[..., None]
    buf_tm = _pad_ff_halves(jnp.swapaxes(state_ffn_conv, 1, 2), ffp, F32)
    k_cache = cache_k_win.reshape(depth, nbatch, WINDOW, KV_WIDTH)
    v_cache = cache_v_win.reshape(depth, nbatch, WINDOW, KV_WIDTH)

    xp = x_prompt.reshape(batch * seq, d)
    xs = jnp.swapaxes(x_sample, 0, 1).reshape(t_new * nbatch, d)
    kp_l, vp_l, cp_l, ks_l, vs_l, cs_l, sv_l = [], [], [], [], [], [], []
    for l in range(depth):
        q, k, v, u, vs = _proj(l, xp, gmix, w_in_b, gq2, gk2, lng, lnb)
        xp = _mix_prompt(l, attn_sinks, q, k, v, u, vs, sgu_w, b_s_t, goa, gos, xp, w_o_b, batch, seq)
        xp, sa, sg = _ffn_prompt(l, xp, gffn, w_up_b, cw_p, cb_p, w_down_b, seq)
        kp_l.append(k.reshape(batch, seq, KV_HEADS, HEAD_DIM)[:, -WINDOW:])
        vp_l.append(v.reshape(batch, seq, KV_HEADS, HEAD_DIM)[:, -WINDOW:])
        tiles_per_seq = seq // TM_FFN
        cp_l.append(_unpad_ff_halves(sa, sg, d_ff)[tiles_per_seq - 1:batch * tiles_per_seq:tiles_per_seq])

        q, k, v, u, vs = _proj(l, xs, gmix, w_in_b, gq2, gk2, lng, lnb)
        pad_new = ((0, 0), (0, NEW_PAD - t_new), (0, 0))
        k_new = jnp.pad(jnp.swapaxes(k.reshape(t_new, nbatch, KV_WIDTH), 0, 1), pad_new)
        v_new = jnp.pad(jnp.swapaxes(v.reshape(t_new, nbatch, KV_WIDTH), 0, 1), pad_new)
        q5 = jnp.transpose(q.reshape(t_new, nbatch, ATT_HEADS, HEAD_DIM), (1, 2, 0, 3))
        q5 = q5.reshape(nbatch, ATT_HEADS * t_new, HEAD_DIM)
        q_ext = jnp.concatenate([q5, q5], axis=-1)
        o_ext, k_win, v_win = _attn_sample(l, q_ext, k_cache, k_new, v_cache, v_new, sink_rows, t_new)
        o5 = o_ext.reshape(nbatch, GROUP, t_new, KV_HEADS, HEAD_DIM)
        att = jnp.transpose(o5, (2, 0, 3, 1, 4)).reshape(t_new * nbatch, ATT_WIDTH)
        mix = _merge_sample(l, att, u, vs, wrow, brow, goa, gos, nbatch, t_new)
        xs = _out_proj(l, xs, mix, w_o_b)
        xs, sa, sg = _ffn_sample(l, xs, gffn, w_up_b, cw_p, cb_p, w_down_b, buf_tm, nbatch, t_new)
        ks_l.append(k_win.reshape(nbatch, WINDOW, KV_HEADS, HEAD_DIM))
        vs_l.append(v_win.reshape(nbatch, WINDOW, KV_HEADS, HEAD_DIM))
        cs_l.append(jnp.swapaxes(_unpad_ff_halves(sa, sg, d_ff), 0, 1))
        sv_l.append(jnp.swapaxes(vs.reshape(t_new, nbatch, SGU_HEADS, SGU_HEAD_DIM), 0, 1))

    y_prompt = xp.reshape(batch, seq, d)
    y_sample = jnp.swapaxes(xs.reshape(t_new, nbatch, d), 0, 1)
    return (y_prompt, y_sample, jnp.stack(kp_l), jnp.stack(vp_l), jnp.stack(cp_l),
            jnp.stack(ks_l), jnp.stack(vs_l), jnp.stack(cs_l), jnp.stack(sv_l))
```
